```python
import jax, jax.numpy as jnp
from jax import lax
import numpy as np

D_MODEL = 2048
BATCH = 32
SEQ = 256
DEPTH = 2
DEC_BATCH = 8
DEC_SEQ = 1024
PAST_LEN = 256

GRID_W = 64
RET_HEADS = 8
RET_DK = 128
RET_DV = 256
CHUNK = 128
GM_GROUPS = 4
GM_WIDTH = 1024
GM_GROUP_DIM = GM_WIDTH // GM_GROUPS
NA_HEADS = 8
NA_HEAD_DIM = 128
NA_WIN_ROWS = 8
NA_WIN_COLS = 16
FFN_DIM = 5632
N_SUB = 3
ROPE_BASE = 10000.0
NORM_EPS = 1e-6
Q_BLOCK = 128
NEG_INF = -1e30
RET_QK_W = RET_HEADS * RET_DK
RET_W = RET_HEADS * RET_DV
NA_W = NA_HEADS * NA_HEAD_DIM
IN_SIZES = (RET_QK_W, RET_QK_W, RET_W, RET_W, GM_WIDTH, GM_WIDTH, NA_W, NA_W, NA_W, D_MODEL, D_MODEL, D_MODEL)
IN_WIDTH = 2 * RET_QK_W + 2 * RET_W + 2 * GM_WIDTH + 3 * NA_W + 3 * D_MODEL

kernel_name = "hybrid_dit_retention_gmlp_natten_step"


def rms_norm(x, w):
    xf = x.astype(jnp.float32)
    y = xf * lax.rsqrt(jnp.mean(xf * xf, axis=-1, keepdims=True) + NORM_EPS)
    return (y * w).astype(x.dtype)


def layer_norm(x, w):
    xf = x.astype(jnp.float32)
    mu = jnp.mean(xf, axis=-1, keepdims=True)
    var = jnp.mean(jnp.square(xf - mu), axis=-1, keepdims=True)
    return ((xf - mu) * lax.rsqrt(var + NORM_EPS) * w).astype(x.dtype)


def mod_part(mod, i, j):
    s = (3 * i + j) * D_MODEL
    return mod[..., s:s + D_MODEL]


def modulated_input(x, mod, i, g_pre):
    return rms_norm(x, g_pre) * (1.0 + mod_part(mod, i, 1)) + mod_part(mod, i, 0)


def gated_residual(x, mod, i, out, g_post, res_w):
    return x + res_w * mod_part(mod, i, 2) * rms_norm(out, g_post)


def swiglu(h, w_in, w_out):
    a, b = jnp.split(h @ w_in, 2, axis=-1)
    return (jax.nn.silu(a) * b) @ w_out


def ffn_sublayer(x, mod, i, g_pre, g_post, w_in, w_out):
    h = modulated_input(x, mod, i, g_pre)
    return gated_residual(x, mod, i, swiglu(h, w_in, w_out), g_post, 0.5)


def split_projection(z):
    out, start = [], 0
    for size in IN_SIZES:
        out.append(z[..., start:start + size])
        start += size
    return out


def to_heads(x, n_heads):
    return x.reshape(x.shape[0], x.shape[1], n_heads, -1)


def axial_rotary(x):
    L, d = x.shape[1], x.shape[-1]
    t = jnp.arange(L)
    pos_r = (t // GRID_W).astype(jnp.float32)
    pos_c = (t % GRID_W).astype(jnp.float32)
    half = d // 2
    inv = ROPE_BASE ** (-jnp.arange(0, half, 2, dtype=jnp.float32) / half)

    def rotate(xa, pos):
        ang = pos[:, None] * inv[None, :]
        cos = jnp.cos(ang)[None, :, None, :]
        sin = jnp.sin(ang)[None, :, None, :]
        x1, x2 = xa[..., :half // 2], xa[..., half // 2:]
        return jnp.concatenate([x1 * cos - x2 * sin, x1 * sin + x2 * cos], axis=-1)

    return jnp.concatenate([rotate(x[..., :half], pos_r), rotate(x[..., half:], pos_c)], axis=-1).astype(x.dtype)


def retention_dir(q, k, v, log_g, s0):
    Bn, L, H, dk = q.shape
    dv = v.shape[-1]
    n = L // CHUNK
    qc = q.reshape(Bn, n, CHUNK, H, dk)
    kc = k.reshape(Bn, n, CHUNK, H, dk)
    vc = v.reshape(Bn, n, CHUNK, H, dv)
    pos = jnp.arange(CHUNK, dtype=jnp.float32)
    diff = pos[:, None] - pos[None, :]
    dmask = jnp.where(diff >= 0, jnp.exp(log_g[:, None, None] * jnp.maximum(diff, 0.0)), 0.0)
    scores = jnp.einsum('bnihd,bnjhd->bnhij', qc, kc) * dmask[None, None]
    inner = jnp.einsum('bnhij,bnjhe->bnihe', scores, vc)
    zeta = jnp.exp(log_g[:, None] * (CHUNK - 1.0 - pos)[None, :])
    kv = jnp.einsum('bnjhd,hj,bnjhe->bnhde', kc, zeta, vc)
    g_chunk = jnp.exp(log_g * CHUNK)[None, :, None, None]

    def step(s, kv_i):
        return g_chunk * s + kv_i, s

    s_final, s_prev = lax.scan(step, s0.astype(kv.dtype), jnp.moveaxis(kv, 1, 0))
    s_prev = jnp.moveaxis(s_prev, 0, 1)
    xi = jnp.exp(log_g[:, None] * (pos + 1.0)[None, :])
    cross = jnp.einsum('bnihd,bnhde->bnihe', qc, s_prev) * xi.T[None, None, :, :, None]
    return (inner + cross).reshape(Bn, L, H, dv), s_final


def retention_branch(q, k, v, g, log_g, s0_f, s0_b):
    Bn, L = q.shape[0], q.shape[1]
    k = k * (RET_DK ** -0.5)
    o_f, s_f = retention_dir(q, k, v, log_g[0], s0_f)
    o_b, s_b = retention_dir(jnp.flip(q, 1), jnp.flip(k, 1), jnp.flip(v, 1), log_g[1], s0_b)
    o = (o_f + jnp.flip(o_b, 1)).astype(jnp.float32)
    o = o * lax.rsqrt(jnp.mean(o * o, axis=-1, keepdims=True) + NORM_EPS)
    o = o.astype(q.dtype) * jax.nn.silu(g.reshape(Bn, L, RET_HEADS, RET_DV))
    return o.reshape(Bn, L, RET_W), s_f, s_b


def spatial_gating(u, v, gm_norm, gm_ws, gm_bs):
    Bn, L, _ = u.shape
    n = L // CHUNK
    vn = layer_norm(v, gm_norm).reshape(Bn, n, CHUNK, GM_GROUPS, GM_GROUP_DIM)
    mixed = jnp.einsum('gij,bnjgc->bnigc', gm_ws, vn) + gm_bs.T[None, None, :, :, None]
    return u * mixed.reshape(Bn, L, GM_WIDTH)


def dense_attention(q, k, v):
    Bn, Lq, H, d = q.shape
    scale = d ** -0.5
    qb = jnp.moveaxis(q.reshape(Bn, Lq // Q_BLOCK, Q_BLOCK, H, d), 1, 0)

    def one_block(qblk):
        s = jnp.einsum('bqhd,bkhd->bhqk', qblk, k).astype(jnp.float32) * scale
        p = jax.nn.softmax(s, axis=-1).astype(v.dtype)
        return jnp.einsum('bhqk,bkhd->bqhd', p, v)

    o = lax.map(one_block, qb)
    return jnp.moveaxis(o, 0, 1).reshape(Bn, Lq, H * d)


def neighbourhood_attention(q, k, v, k_ctx, v_ctx, rpb):
    Bn, L, H, d = q.shape
    rows = L // GRID_W
    wr = min(NA_WIN_ROWS, rows)
    scale = d ** -0.5
    r = jnp.arange(rows)
    row_start = jnp.clip(r - wr // 2, 0, rows - wr)
    row_idx = row_start[:, None] + jnp.arange(wr)[None, :]
    qg = q.reshape(Bn, rows, GRID_W, H, d)
    kb = jnp.take(k.reshape(Bn, rows, GRID_W, H, d), row_idx, axis=1)
    vb = jnp.take(v.reshape(Bn, rows, GRID_W, H, d), row_idx, axis=1)
    s_loc = jnp.einsum('brqhd,brjkhd->brhqjk', qg, kb).astype(jnp.float32) * scale
    cidx = jnp.arange(GRID_W)
    col_start = jnp.clip(cidx - NA_WIN_COLS // 2, 0, GRID_W - NA_WIN_COLS)
    col_valid = (cidx[None, :] >= col_start[:, None]) & (cidx[None, :] < col_start[:, None] + NA_WIN_COLS)
    col_i = jnp.clip(cidx[None, :] - cidx[:, None] + NA_WIN_COLS - 1, 0, 2 * NA_WIN_COLS - 2)
    row_off = row_idx - r[:, None] + NA_WIN_ROWS - 1
    bias = rpb[:, row_off][..., col_i]
    bias = jnp.transpose(bias, (1, 0, 3, 2, 4)).astype(jnp.float32)
    s_loc = jnp.where(col_valid[:, None, :], s_loc + bias[None], NEG_INF)
    s_ctx = jnp.einsum('brqhd,bkhd->brhqk', qg, k_ctx).astype(jnp.float32) * scale
    n_loc = wr * GRID_W
    s_all = jnp.concatenate([s_loc.reshape(Bn, rows, H, GRID_W, n_loc), s_ctx], axis=-1)
    p = jax.nn.softmax(s_all, axis=-1).astype(v.dtype)
    p_loc = p[..., :n_loc].reshape(Bn, rows, H, GRID_W, wr, GRID_W)
    p_ctx = p[..., n_loc:]
    o = jnp.einsum('brhqjk,brjkhd->brqhd', p_loc, vb) + jnp.einsum('brhqk,bkhd->brqhd', p_ctx, v_ctx)
    return o.reshape(Bn, L, H * d)


def merge_branches(ret, gm, na, g_a, g_b, g_c, w_br, w_bg, w_bn, w_out):
    m = jax.nn.sigmoid(g_a) * (ret @ w_br) + jax.nn.sigmoid(g_b) * (gm @ w_bg) + jax.nn.sigmoid(g_c) * (na @ w_bn)
    return m @ w_out


def mix_context(h, w_in, ret_logit, gm_norm, gm_ws, gm_bs, w_br, w_bg, w_bn, w_out):
    Bn = h.shape[0]
    qa, ka, va, ga, ub, vb, qc, kc, vc, g_a, g_b, g_c = split_projection(h @ w_in)
    log_g = jax.nn.log_sigmoid(ret_logit.astype(jnp.float32))
    s0 = jnp.zeros((Bn, RET_HEADS, RET_DK, RET_DV), jnp.float32)
    ret, s_f, s_b = retention_branch(to_heads(qa, RET_HEADS), to_heads(ka, RET_HEADS), to_heads(va, RET_HEADS), ga, log_g, s0, s0)
    gm = spatial_gating(ub, vb, gm_norm, gm_ws, gm_bs)
    k_c = to_heads(kc, NA_HEADS)
    v_c = to_heads(vc, NA_HEADS)
    na = dense_attention(to_heads(qc, NA_HEADS), k_c, v_c)
    y = merge_branches(ret, gm, na, g_a, g_b, g_c, w_br, w_bg, w_bn, w_out)
    return y, k_c, v_c, jnp.stack([s_f, s_b], axis=1)


def mix_latent(h, ctx_k, ctx_v, ret_state, w_in, ret_logit, gm_norm, gm_ws, gm_bs, rpb, w_br, w_bg, w_bn, w_out):
    qa, ka, va, ga, ub, vb, qc, kc, vc, g_a, g_b, g_c = split_projection(h @ w_in)
    log_g = jax.nn.log_sigmoid(ret_logit.astype(jnp.float32))
    ret, _, _ = retention_branch(axial_rotary(to_heads(qa, RET_HEADS)), axial_rotary(to_heads(ka, RET_HEADS)),
                                 to_heads(va, RET_HEADS), ga, log_g, ret_state[:, 0], ret_state[:, 1])
    gm = spatial_gating(ub, vb, gm_norm, gm_ws, gm_bs)
    na = neighbourhood_attention(to_heads(qc, NA_HEADS), to_heads(kc, NA_HEADS), to_heads(vc, NA_HEADS), ctx_k, ctx_v, rpb)
    return merge_branches(ret, gm, na, g_a, g_b, g_c, w_br, w_bg, w_bn, w_out)


def setup_inputs(seed: int = 0) -> dict:
    key = jax.random.key(seed)
    ks = jax.random.split(key, 24)
    D = D_MODEL

    def nrm(k, shape, s):
        return jax.random.normal(k, shape, jnp.float32) * s

    gamma0 = 1.0 - 2.0 ** (-5.0 - jnp.arange(RET_HEADS, dtype=jnp.float32))
    logit0 = jnp.log(gamma0) - jnp.log1p(-gamma0)
    return {
        "x_prompt": nrm(ks[0], (BATCH, SEQ, D), 1.0),
        "x_sample": nrm(ks[1], (DEC_BATCH, DEC_SEQ, D), 1.0),
        "cache_k": nrm(ks[2], (DEC_BATCH, DEPTH, PAST_LEN, NA_HEADS, NA_HEAD_DIM), 1.0),
        "cache_v": nrm(ks[3], (DEC_BATCH, DEPTH, PAST_LEN, NA_HEADS, NA_HEAD_DIM), 1.0),
        "state_ret": nrm(ks[4], (DEC_BATCH, DEPTH, 2, RET_HEADS, RET_DK, RET_DV), 1.0),
        "c": nrm(ks[5], (DEC_BATCH, D), 1.0),
        "c_ctx": nrm(ks[6], (D,), 1.0),
        "w_ada": nrm(ks[7], (DEPTH, D, 3 * N_SUB * D), 0.5 * D ** -0.5),
        "b_ada": nrm(ks[8], (DEPTH, 3 * N_SUB * D), 0.02),
        "norm_pre": 1.0 + nrm(ks[9], (DEPTH, N_SUB, D), 0.05),
        "norm_post": 1.0 + nrm(ks[10], (DEPTH, N_SUB, D), 0.05),
        "ffn_w_in": nrm(ks[11], (DEPTH, 2, D, 2 * FFN_DIM), D ** -0.5),
        "ffn_w_out": nrm(ks[12], (DEPTH, 2, FFN_DIM, D), FFN_DIM ** -0.5),
        "w_in": nrm(ks[13], (DEPTH, D, IN_WIDTH), D ** -0.5),
        "ret_decay_logit": logit0[None, None, :] + nrm(ks[14], (DEPTH, 2, RET_HEADS), 0.1),
        "gm_norm": 1.0 + nrm(ks[15], (DEPTH, GM_WIDTH), 0.05),
        "gm_ws": nrm(ks[16], (DEPTH, GM_GROUPS, CHUNK, CHUNK), CHUNK ** -0.5),
        "gm_bs": nrm(ks[17], (DEPTH, GM_GROUPS, CHUNK), 0.02),
        "na_rpb": nrm(ks[18], (DEPTH, NA_HEADS, 2 * NA_WIN_ROWS - 1, 2 * NA_WIN_COLS - 1), 0.1),
        "w_branch_ret": nrm(ks[19], (DEPTH, RET_W, D), RET_W ** -0.5),
        "w_branch_gm": nrm(ks[20], (DEPTH, GM_WIDTH, D), GM_WIDTH ** -0.5),
        "w_branch_na": nrm(ks[21], (DEPTH, NA_W, D), NA_W ** -0.5),
        "w_out": nrm(ks[22], (DEPTH, D, D), D ** -0.5),
    }


def reference(x_prompt, x_sample, cache_k, cache_v, state_ret, c, c_ctx,
              w_ada, b_ada, norm_pre, norm_post, ffn_w_in, ffn_w_out, w_in,
              ret_decay_logit, gm_norm, gm_ws, gm_bs, na_rpb,
              w_branch_ret, w_branch_gm, w_branch_na, w_out):
    x = x_prompt
    ks_out, vs_out, ss_out = [], [], []
    for l in range(DEPTH):
        mod = (jax.nn.silu(c_ctx) @ w_ada[l] + b_ada[l])[None, None, :]
        x = ffn_sublayer(x, mod, 0, norm_pre[l, 0], norm_post[l, 0], ffn_w_in[l, 0], ffn_w_out[l, 0])
        h = modulated_input(x, mod, 1, norm_pre[l, 1])
        y, k_c, v_c, s_c = mix_context(h, w_in[l], ret_decay_logit[l], gm_norm[l], gm_ws[l], gm_bs[l],
                                       w_branch_ret[l], w_branch_gm[l], w_branch_na[l], w_out[l])
        x = gated_residual(x, mod, 1, y, norm_post[l, 1], 1.0)
        x = ffn_sublayer(x, mod, 2, norm_pre[l, 2], norm_post[l, 2], ffn_w_in[l, 1], ffn_w_out[l, 1])
        ks_out.append(k_c)
        vs_out.append(v_c)
        ss_out.append(s_c)
    y_prompt = x
    new_cache_k = jnp.stack(ks_out, axis=1)
    new_cache_v = jnp.stack(vs_out, axis=1)
    new_state_ret = jnp.stack(ss_out, axis=1)

    x = x_sample
    for l in range(DEPTH):
        mod = (jax.nn.silu(c) @ w_ada[l] + b_ada[l])[:, None, :]
        x = ffn_sublayer(x, mod, 0, norm_pre[l, 0], norm_post[l, 0], ffn_w_in[l, 0], ffn_w_out[l, 0])
        h = modulated_input(x, mod, 1, norm_pre[l, 1])
        y = mix_latent(h, cache_k[:, l], cache_v[:, l], state_ret[:, l], w_in[l], ret_decay_logit[l],
                       gm_norm[l], gm_ws[l], gm_bs[l], na_rpb[l],
                       w_branch_ret[l], w_branch_gm[l], w_branch_na[l], w_out[l])
        x = gated_residual(x, mod, 1, y, norm_post[l, 1], 1.0)
        x = ffn_sublayer(x, mod, 2, norm_pre[l, 2], norm_post[l, 2], ffn_w_in[l, 1], ffn_w_out[l, 1])
    y_sample = x
    return (y_prompt, y_sample, new_cache_k, new_cache_v, new_state_ret)
```

```python
import functools

import jax
import jax.numpy as jnp
from jax import lax
from jax.experimental import pallas as pl
from jax.experimental.pallas import tpu as pltpu

D_MODEL = 2048
GRID_W = 64
RET_HEADS = 8
RET_DK = 128
RET_DV = 256
CHUNK = 128
GM_GROUPS = 4
GM_WIDTH = 1024
GM_GROUP_DIM = GM_WIDTH // GM_GROUPS
NA_HEADS = 8
NA_HEAD_DIM = 128
NA_WIN_ROWS = 8
NA_WIN_COLS = 16
N_SUB = 3
ROPE_BASE = 10000.0
NORM_EPS = 1e-6
NEG_INF = -1e30
NA_W = NA_HEADS * NA_HEAD_DIM

BF16 = jnp.bfloat16
F32 = jnp.float32

V7X_VMEM_LIMIT_BYTES = 56 * 1024 * 1024
ROW_TILE = 512
PROJ_COL_TILE = 1024
MOD_ROWS = 16
T_QA, T_KA, T_VA, T_GA, T_UB, T_VB, T_QC, T_KC, T_VC, T_GATES, T_END = 0, 1, 2, 4, 6, 7, 8, 9, 10, 11, 17


def _cparams(n_axes):
    return pltpu.CompilerParams(dimension_semantics=("arbitrary",) * n_axes,
                                vmem_limit_bytes=V7X_VMEM_LIMIT_BYTES)


def _silu(x):
    return x * jax.nn.sigmoid(x)


def _dot(a, b):
    return jnp.dot(a, b, preferred_element_type=F32)


def _dot_nt(a, b):
    return lax.dot_general(a, b, (((1,), (1,)), ((), ())), preferred_element_type=F32)


def _dot_tn(a, b):
    return lax.dot_general(a, b, (((0,), (0,)), ((), ())), preferred_element_type=F32)


def _rms(x):
    return x * lax.rsqrt(jnp.mean(x * x, axis=-1, keepdims=True) + NORM_EPS)


def _modulated_input(x, g_pre, scale, shift):
    return (_rms(x) * g_pre) * (1.0 + scale) + shift


def _adaln_kernel(c_ref, w_ref, b_ref, o_ref):
    s = _silu(c_ref[...]).astype(BF16)
    o_ref[...] = _dot(s, w_ref[...].astype(BF16)) + b_ref[...]


def _adaln(cond, w_ada, b_ada):
    depth, d, n = w_ada.shape
    tn = 1024
    return pl.pallas_call(
        _adaln_kernel,
        grid=(depth, n // tn),
        in_specs=[pl.BlockSpec((MOD_ROWS, d), lambda l, j: (0, 0)),
                  pl.BlockSpec((None, d, tn), lambda l, j: (l, 0, j)),
                  pl.BlockSpec((None, 1, tn), lambda l, j: (l, 0, j))],
        out_specs=pl.BlockSpec((None, MOD_ROWS, tn), lambda l, j: (l, 0, j)),
        out_shape=jax.ShapeDtypeStruct((depth, MOD_ROWS, n), F32),
        compiler_params=_cparams(2),
    )(cond, w_ada, b_ada.reshape(depth, 1, n))


def _mod_spec(n_ctx_rows, lat_seq, tm, sub, part):
    nc = n_ctx_rows // tm
    per = lat_seq // tm

    def index(i, *_):
        row = jnp.where(i < nc, 0, 1 + (i - nc) // per)
        return (row * (3 * N_SUB) + 3 * sub + part, 0, 0)

    return pl.BlockSpec((None, 1, D_MODEL), index)


def _vec_spec():
    return pl.BlockSpec((1, D_MODEL), lambda *_: (0, 0))


def _prenorm_kernel(x_ref, g_ref, sc_ref, sh_ref, h_ref):
    h_ref[...] = _modulated_input(x_ref[...], g_ref[...], sc_ref[...], sh_ref[...]).astype(BF16)


def _prenorm(x, g_pre, mod_tab, dims):
    n_ctx_rows, lat_seq = dims
    rows = x.shape[0]
    tm = ROW_TILE
    return pl.pallas_call(
        _prenorm_kernel,
        grid=(rows // tm,),
        in_specs=[pl.BlockSpec((tm, D_MODEL), lambda i: (i, 0)),
                  _vec_spec(),
                  _mod_spec(n_ctx_rows, lat_seq, tm, 0, 1),
                  _mod_spec(n_ctx_rows, lat_seq, tm, 0, 0)],
        out_specs=pl.BlockSpec((tm, D_MODEL), lambda i: (i, 0)),
        out_shape=jax.ShapeDtypeStruct((rows, D_MODEL), BF16),
        compiler_params=_cparams(1),
    )(x, g_pre.reshape(1, D_MODEL), mod_tab, mod_tab)


def _ffn_up_kernel(h_ref, wa_ref, wb_ref, o_ref, wa_s, wb_s):
    @pl.when(pl.program_id(1) == 0)
    def _():
        wa_s[...] = wa_ref[...].astype(BF16)
        wb_s[...] = wb_ref[...].astype(BF16)

    h = h_ref[...]
    a = _dot(h, wa_s[...])
    b = _dot(h, wb_s[...])
    o_ref[...] = (_silu(a) * b).astype(BF16)


def _ffn_up(h, w_in):
    rows, d = h.shape
    f = w_in.shape[1] // 2
    tm, tf = ROW_TILE, 512
    nf = f // tf
    return pl.pallas_call(
        _ffn_up_kernel,
        grid=(nf, rows // tm),
        in_specs=[pl.BlockSpec((tm, d), lambda j, i: (i, 0)),
                  pl.BlockSpec((d, tf), lambda j, i: (0, j)),
                  pl.BlockSpec((d, tf), lambda j, i: (0, nf + j))],
        out_specs=pl.BlockSpec((tm, tf), lambda j, i: (i, j)),
        out_shape=jax.ShapeDtypeStruct((rows, f), BF16),
        scratch_shapes=[pltpu.VMEM((d, tf), BF16), pltpu.VMEM((d, tf), BF16)],
        compiler_params=_cparams(2),
    )(h, w_in, w_in)


def _rowout_kernel(*refs, res_w, n_k, with_next):
    if with_next:
        a_ref, w_ref, x_ref, gpost_ref, gate_ref, gpre_ref, sc_ref, sh_ref, xo_ref, ho_ref, acc = refs
    else:
        a_ref, w_ref, x_ref, gpost_ref, gate_ref, xo_ref, acc = refs
    k = pl.program_id(1)

    @pl.when(k == 0)
    def _():
        acc[...] = jnp.zeros_like(acc)

    acc[...] += _dot(a_ref[...], w_ref[...].astype(BF16))

    @pl.when(k == n_k - 1)
    def _():
        xn = x_ref[...] + (res_w * gate_ref[...]) * (_rms(acc[...]) * gpost_ref[...])
        xo_ref[...] = xn
        if with_next:
            ho_ref[...] = _modulated_input(xn, gpre_ref[...], sc_ref[...], sh_ref[...]).astype(BF16)


def _rowout(a, w, x, mod_tab, sub, res_w, g_post, dims, nxt=None):
    n_ctx_rows, lat_seq = dims
    rows, kdim = a.shape
    tm, tk = ROW_TILE, 512
    n_k = kdim // tk
    row_spec = pl.BlockSpec((tm, D_MODEL), lambda i, k: (i, 0))
    in_specs = [pl.BlockSpec((tm, tk), lambda i, k: (i, k)),
                pl.BlockSpec((tk, D_MODEL), lambda i, k: (k, 0)),
                row_spec, _vec_spec(), _mod_spec(n_ctx_rows, lat_seq, tm, sub, 2)]
    args = [a, w, x, g_post.reshape(1, D_MODEL), mod_tab]
    out_specs = [row_spec]
    out_shape = [jax.ShapeDtypeStruct((rows, D_MODEL), F32)]
    if nxt is not None:
        g_pre, mod_next, sub_next = nxt
        in_specs += [_vec_spec(), _mod_spec(n_ctx_rows, lat_seq, tm, sub_next, 1),
                     _mod_spec(n_ctx_rows, lat_seq, tm, sub_next, 0)]
        args += [g_pre.reshape(1, D_MODEL), mod_next, mod_next]
        out_specs.append(row_spec)
        out_shape.append(jax.ShapeDtypeStruct((rows, D_MODEL), BF16))
    out = pl.pallas_call(
        functools.partial(_rowout_kernel, res_w=res_w, n_k=n_k, with_next=nxt is not None),
        grid=(rows // tm, n_k),
        in_specs=in_specs,
        out_specs=out_specs,
        out_shape=out_shape,
        scratch_shapes=[pltpu.VMEM((tm, D_MODEL), F32)],
        compiler_params=_cparams(2),
    )(*args)
    return (out[0], out[1]) if nxt is not None else (out[0], None)


def _proj_kernel(h_ref, w_ref, o_ref, w_s):
    @pl.when(pl.program_id(1) == 0)
    def _():
        w_s[...] = w_ref[...].astype(BF16)

    o_ref[...] = _dot(h_ref[...], w_s[...]).astype(o_ref.dtype)


def _proj(h, w, tile0, n_tiles, out_dtype):
    rows, d = h.shape
    tm, tn = ROW_TILE, PROJ_COL_TILE
    return pl.pallas_call(
        _proj_kernel,
        grid=(n_tiles, rows // tm),
        in_specs=[pl.BlockSpec((tm, d), lambda j, i: (i, 0)),
                  pl.BlockSpec((d, tn), lambda j, i: (0, tile0 + j))],
        out_specs=pl.BlockSpec((tm, tn), lambda j, i: (i, j)),
        out_shape=jax.ShapeDtypeStruct((rows, n_tiles * tn), out_dtype),
        scratch_shapes=[pltpu.VMEM((d, tn), BF16)],
        compiler_params=_cparams(2),
    )(h, w)


def _log_sigmoid(x):
    return jnp.minimum(x, 0.0) - jnp.log1p(jnp.exp(-jnp.abs(x)))


def _retention_kernel(*refs, n_chunks, latent):
    if latent:
        (logit_ref, q_ref, k_ref, v_ref, g_ref, cos_ref, sa_ref, sb_ref, s0_ref, _,
         o_ref, q_s, k_s, kf_s, kb_s, sf_s, sb_s) = refs
    else:
        (logit_ref, q_ref, k_ref, v_ref, g_ref,
         o_ref, so_ref, q_s, k_s, kf_s, kb_s, sf_s, sb_s) = refs
    c = CHUNK
    lg = _log_sigmoid(logit_ref[...])
    lgf, lgb = lg[0], lg[1]
    lgf_k, lgb_k = lgf[:, :RET_DK], lgb[:, :RET_DK]

    ii = lax.broadcasted_iota(jnp.int32, (c, c), 0)
    jj = lax.broadcasted_iota(jnp.int32, (c, c), 1)
    dist = (ii - jj).astype(F32)
    dmask = (jnp.where(dist >= 0, jnp.exp(lgf_k * jnp.maximum(dist, 0.0)), 0.0)
             + jnp.where(dist <= 0, jnp.exp(lgb_k * jnp.maximum(-dist, 0.0)), 0.0))
    pos_k = ii.astype(F32)
    zeta_f = jnp.exp(lgf_k * (c - 1.0 - pos_k))
    zeta_b = jnp.exp(lgb_k * pos_k)
    pos_v = lax.broadcasted_iota(jnp.int32, (c, RET_DV), 0).astype(F32)
    xi_f = jnp.exp(lgf * (pos_v + 1.0))
    xi_b = jnp.exp(lgb * (c - pos_v))
    decay_f = jnp.exp(lgf * c)
    decay_b = jnp.exp(lgb * c)

    for n in range(n_chunks):
        sl = slice(n * c, (n + 1) * c)
        q = q_ref[sl, :].astype(F32)
        k = k_ref[sl, :].astype(F32)
        if latent:
            cos, sa, sb = cos_ref[sl, :], sa_ref[sl, :], sb_ref[sl, :]
            q = q * cos + pltpu.roll(q, 96, 1) * sa + pltpu.roll(q, 32, 1) * sb
            k = k * cos + pltpu.roll(k, 96, 1) * sa + pltpu.roll(k, 32, 1) * sb
        k = k * (RET_DK ** -0.5)
        q_s[sl, :] = q.astype(BF16)
        k_s[sl, :] = k.astype(BF16)
        kf_s[sl, :] = (k * zeta_f).astype(BF16)
        kb_s[sl, :] = (k * zeta_b).astype(BF16)

    s = s0_ref[0] if latent else jnp.zeros((RET_DK, RET_DV), F32)
    for n in range(n_chunks):
        sl = slice(n * c, (n + 1) * c)
        sf_s[n] = s.astype(BF16)
        s = decay_f * s + _dot_tn(kf_s[sl, :], v_ref[sl, :])
    if not latent:
        so_ref[0] = s
    s = s0_ref[1] if latent else jnp.zeros((RET_DK, RET_DV), F32)
    for n in reversed(range(n_chunks)):
        sl = slice(n * c, (n + 1) * c)
        sb_s[n] = s.astype(BF16)
        s = decay_b * s + _dot_tn(kb_s[sl, :], v_ref[sl, :])
    if not latent:
        so_ref[1] = s

    for n in range(n_chunks):
        sl = slice(n * c, (n + 1) * c)
        q = q_s[sl, :]
        scores = (_dot_nt(q, k_s[sl, :]) * dmask).astype(BF16)
        o = (_dot(scores, v_ref[sl, :])
             + _dot(q, sf_s[n]) * xi_f + _dot(q, sb_s[n]) * xi_b)
        o = _rms(o) * _silu(g_ref[sl, :].astype(F32))
        o_ref[sl, :] = o.astype(BF16)


def _retention_scratch(seq):
    n_chunks = seq // CHUNK
    return [pltpu.VMEM((seq, RET_DK), BF16)] * 4 + [pltpu.VMEM((n_chunks, RET_DK, RET_DV), BF16)] * 2


def _retention_ctx(z, logits, n_seq, seq):
    rows = z.shape[0]
    kq, kk = T_QA * PROJ_COL_TILE // RET_DK, T_KA * PROJ_COL_TILE // RET_DK
    kv, kg = T_VA * PROJ_COL_TILE // RET_DV, T_GA * PROJ_COL_TILE // RET_DV
    return pl.pallas_call(
        functools.partial(_retention_kernel, n_chunks=seq // CHUNK, latent=False),
        grid=(n_seq, RET_HEADS),
        in_specs=[pl.BlockSpec((None, 2, 1, RET_DV), lambda b, h: (h, 0, 0, 0)),
                  pl.BlockSpec((seq, RET_DK), lambda b, h: (b, kq + h)),
                  pl.BlockSpec((seq, RET_DK), lambda b, h: (b, kk + h)),
                  pl.BlockSpec((seq, RET_DV), lambda b, h: (b, kv + h)),
                  pl.BlockSpec((seq, RET_DV), lambda b, h: (b, kg + h))],
        out_specs=[pl.BlockSpec((seq, RET_DV), lambda b, h: (b, h)),
                   pl.BlockSpec((None, 2, None, RET_DK, RET_DV), lambda b, h: (b, 0, h, 0, 0))],
        out_shape=[jax.ShapeDtypeStruct((rows, RET_HEADS * RET_DV), BF16),
                   jax.ShapeDtypeStruct((n_seq, 2, RET_HEADS, RET_DK, RET_DV), F32)],
        scratch_shapes=_retention_scratch(seq),
        compiler_params=_cparams(2),
    )(logits, z, z, z, z)


def _retention_latent(z, logits, rot, state, layer, ret_all, n_seq, seq, row0):
    kq, kk = T_QA * PROJ_COL_TILE // RET_DK, T_KA * PROJ_COL_TILE // RET_DK
    kv, kg = T_VA * PROJ_COL_TILE // RET_DV, T_GA * PROJ_COL_TILE // RET_DV
    r0 = row0 // seq
    tab_spec = pl.BlockSpec((seq, RET_DK), lambda b, h: (0, 0))
    return pl.pallas_call(
        functools.partial(_retention_kernel, n_chunks=seq // CHUNK, latent=True),
        grid=(n_seq, RET_HEADS),
        in_specs=[pl.BlockSpec((None, 2, 1, RET_DV), lambda b, h: (h, 0, 0, 0)),
                  pl.BlockSpec((seq, RET_DK), lambda b, h: (r0 + b, kq + h)),
                  pl.BlockSpec((seq, RET_DK), lambda b, h: (r0 + b, kk + h)),
                  pl.BlockSpec((seq, RET_DV), lambda b, h: (r0 + b, kv + h)),
                  pl.BlockSpec((seq, RET_DV), lambda b, h: (r0 + b, kg + h)),
                  tab_spec, tab_spec, tab_spec,
                  pl.BlockSpec((None, None, 2, None, RET_DK, RET_DV), lambda b, h: (b, layer, 0, h, 0, 0)),
                  pl.BlockSpec(memory_space=pl.ANY)],
        out_specs=pl.BlockSpec((seq, RET_DV), lambda b, h: (r0 + b, h)),
        out_shape=jax.ShapeDtypeStruct(ret_all.shape, BF16),
        scratch_shapes=_retention_scratch(seq),
        input_output_aliases={9: 0},
        compiler_params=_cparams(2),
    )(logits, z, z, z, z, rot[0], rot[1], rot[2], state, ret_all)


def _gmlp_kernel(u_ref, v_ref, gn_ref, ws_ref, bs_ref, o_ref, *, n_chunks):
    v = v_ref[...].astype(F32)
    mu = jnp.mean(v, axis=-1, keepdims=True)
    vc = v - mu
    var = jnp.mean(vc * vc, axis=-1, keepdims=True)
    vn = (vc * lax.rsqrt(var + NORM_EPS) * gn_ref[...]).astype(BF16)
    for g in range(GM_GROUPS):
        w = ws_ref[g].astype(BF16)
        cols = slice(g * GM_GROUP_DIM, (g + 1) * GM_GROUP_DIM)
        for n in range(n_chunks):
            rows = slice(n * CHUNK, (n + 1) * CHUNK)
            mixed = _dot(w, vn[rows, cols]) + bs_ref[g]
            o_ref[rows, cols] = (u_ref[rows, cols].astype(F32) * mixed).astype(BF16)


def _gmlp(z, gm_norm, gm_ws, gm_bs):
    rows = z.shape[0]
    tm = ROW_TILE
    return pl.pallas_call(
        functools.partial(_gmlp_kernel, n_chunks=tm // CHUNK),
        grid=(rows // tm,),
        in_specs=[pl.BlockSpec((tm, GM_WIDTH), lambda i: (i, T_UB * PROJ_COL_TILE // GM_WIDTH)),
                  pl.BlockSpec((tm, GM_WIDTH), lambda i: (i, T_VB * PROJ_COL_TILE // GM_WIDTH)),
                  pl.BlockSpec((1, GM_WIDTH), lambda i: (0, 0)),
                  pl.BlockSpec((GM_GROUPS, CHUNK, CHUNK), lambda i: (0, 0, 0)),
                  pl.BlockSpec((GM_GROUPS, CHUNK, 1), lambda i: (0, 0, 0))],
        out_specs=pl.BlockSpec((tm, GM_WIDTH), lambda i: (i, 0)),
        out_shape=jax.ShapeDtypeStruct((rows, GM_WIDTH), BF16),
        compiler_params=_cparams(1),
    )(z, z, gm_norm.reshape(1, GM_WIDTH), gm_ws, gm_bs.reshape(GM_GROUPS, CHUNK, 1))


def _softmax_rows(parts):
    m = functools.reduce(jnp.maximum, [jnp.max(s, axis=-1, keepdims=True) for s in parts])
    e = [jnp.exp(s - m) for s in parts]
    denom = functools.reduce(lambda a, b: a + b, [jnp.sum(x, axis=-1, keepdims=True) for x in e])
    return [(x / denom).astype(BF16) for x in e]


def _attn_ctx_kernel(q_ref, kv_ref, o_ref):
    scale = NA_HEAD_DIM ** -0.5
    for h in range(NA_HEADS):
        cols = slice(h * NA_HEAD_DIM, (h + 1) * NA_HEAD_DIM)
        k = kv_ref[:, cols].astype(BF16)
        v = kv_ref[:, NA_W + h * NA_HEAD_DIM:NA_W + (h + 1) * NA_HEAD_DIM].astype(BF16)
        (p,) = _softmax_rows([_dot_nt(q_ref[:, cols], k) * scale])
        o_ref[:, cols] = _dot(p, v).astype(BF16)


def _attn_ctx(z, kv, n_seq, seq):
    rows = z.shape[0]
    return pl.pallas_call(
        _attn_ctx_kernel,
        grid=(n_seq,),
        in_specs=[pl.BlockSpec((seq, NA_W), lambda b: (b, T_QC * PROJ_COL_TILE // NA_W)),
                  pl.BlockSpec((seq, 2 * NA_W), lambda b: (b, 0))],
        out_specs=pl.BlockSpec((seq, NA_W), lambda b: (b, 0)),
        out_shape=jax.ShapeDtypeStruct((rows, NA_W), BF16),
        compiler_params=_cparams(1),
    )(z, kv)


def _attn_lat_kernel(q_ref, k_ref, v_ref, kc_ref, vc_ref, bias_ref, _, o_ref, *, q_block):
    scale = NA_HEAD_DIM ** -0.5
    k = k_ref[...].astype(BF16)
    v = v_ref[...].astype(BF16)
    kc = kc_ref[...].astype(BF16)
    vc = vc_ref[...].astype(BF16)
    for n in range(q_ref.shape[0] // q_block):
        rows = slice(n * q_block, (n + 1) * q_block)
        q = q_ref[rows, :]
        s_loc = _dot_nt(q, k) * scale + bias_ref[rows, :]
        s_ctx = _dot_nt(q, kc) * scale
        p_loc, p_ctx = _softmax_rows([s_loc, s_ctx])
        o_ref[rows, :] = (_dot(p_loc, v) + _dot(p_ctx, vc)).astype(BF16)


def _attn_latent(z, kv, cache_k, cache_v, bias, layer, na_all, n_seq, seq, row0):
    r0 = row0 // seq
    past = cache_k.shape[2]
    kq = T_QC * PROJ_COL_TILE // NA_HEAD_DIM
    cache_spec = pl.BlockSpec((None, None, past, NA_HEAD_DIM), lambda h, b: (b, layer, 0, h))
    return pl.pallas_call(
        functools.partial(_attn_lat_kernel, q_block=256),
        grid=(NA_HEADS, n_seq),
        in_specs=[pl.BlockSpec((seq, NA_HEAD_DIM), lambda h, b: (r0 + b, kq + h)),
                  pl.BlockSpec((seq, NA_HEAD_DIM), lambda h, b: (r0 + b, h)),
                  pl.BlockSpec((seq, NA_HEAD_DIM), lambda h, b: (r0 + b, NA_HEADS + h)),
                  cache_spec, cache_spec,
                  pl.BlockSpec((None, seq, seq), lambda h, b: (h, 0, 0)),
                  pl.BlockSpec(memory_space=pl.ANY)],
        out_specs=pl.BlockSpec((seq, NA_HEAD_DIM), lambda h, b: (r0 + b, h)),
        out_shape=jax.ShapeDtypeStruct(na_all.shape, BF16),
        input_output_aliases={6: 0},
        compiler_params=_cparams(2),
    )(z, kv, kv, cache_k.reshape(cache_k.shape[:3] + (NA_W,)), cache_v.reshape(cache_v.shape[:3] + (NA_W,)),
      bias, na_all)


def _window_bias(rpb, seq):
    rows = seq // GRID_W
    wr = min(NA_WIN_ROWS, rows)
    t = jnp.arange(seq)
    r, c = t // GRID_W, t % GRID_W
    row_start = jnp.clip(r - wr // 2, 0, rows - wr)
    col_start = jnp.clip(c - NA_WIN_COLS // 2, 0, GRID_W - NA_WIN_COLS)
    row_ok = (r[None, :] >= row_start[:, None]) & (r[None, :] < row_start[:, None] + wr)
    col_ok = (c[None, :] >= col_start[:, None]) & (c[None, :] < col_start[:, None] + NA_WIN_COLS)
    row_off = jnp.clip(r[None, :] - r[:, None] + NA_WIN_ROWS - 1, 0, 2 * NA_WIN_ROWS - 2)
    col_off = jnp.clip(c[None, :] - c[:, None] + NA_WIN_COLS - 1, 0, 2 * NA_WIN_COLS - 2)
    return jnp.where((row_ok & col_ok)[None], rpb[:, row_off, col_off], NEG_INF).astype(F32)


def _rotary_tables(seq):
    t = jnp.arange(seq)
    half = RET_DK // 2
    inv = ROPE_BASE ** (-jnp.arange(0, half, 2, dtype=F32) / half)
    ang_r = (t // GRID_W).astype(F32)[:, None] * inv[None, :]
    ang_c = (t % GRID_W).astype(F32)[:, None] * inv[None, :]
    zero = jnp.zeros_like(ang_r)
    cos = jnp.concatenate([jnp.cos(ang_r), jnp.cos(ang_r), jnp.cos(ang_c), jnp.cos(ang_c)], axis=-1)
    sa = jnp.concatenate([-jnp.sin(ang_r), zero, -jnp.sin(ang_c), zero], axis=-1)
    sb = jnp.concatenate([zero, jnp.sin(ang_r), zero, jnp.sin(ang_c)], axis=-1)
    return cos, sa, sb


def _merge_kernel(ret_ref, gm_ref, na_ref, ga_ref, gb_ref, gc_ref, wr_ref, wg_ref, wn_ref, o_ref, wr_s, wg_s, wn_s):
    @pl.when(pl.program_id(1) == 0)
    def _():
        wr_s[...] = wr_ref[...].astype(BF16)
        wg_s[...] = wg_ref[...].astype(BF16)
        wn_s[...] = wn_ref[...].astype(BF16)

    m = (jax.nn.sigmoid(ga_ref[...].astype(F32)) * _dot(ret_ref[...], wr_s[...])
         + jax.nn.sigmoid(gb_ref[...].astype(F32)) * _dot(gm_ref[...], wg_s[...])
         + jax.nn.sigmoid(gc_ref[...].astype(F32)) * _dot(na_ref[...], wn_s[...]))
    o_ref[...] = m.astype(BF16)


def _merge(ret, gm, na, gates, w_br, w_bg, w_bn):
    rows = ret.shape[0]
    tm, tn = ROW_TILE, 512
    nj = D_MODEL // tn

    def a_spec(width):
        return pl.BlockSpec((tm, width), lambda j, i: (i, 0))

    def gate_spec(which):
        return pl.BlockSpec((tm, tn), lambda j, i: (i, which * nj + j))

    def w_spec(width):
        return pl.BlockSpec((width, tn), lambda j, i: (0, j))

    return pl.pallas_call(
        _merge_kernel,
        grid=(nj, rows // tm),
        in_specs=[a_spec(ret.shape[1]), a_spec(gm.shape[1]), a_spec(na.shape[1]),
                  gate_spec(0), gate_spec(1), gate_spec(2),
                  w_spec(ret.shape[1]), w_spec(gm.shape[1]), w_spec(na.shape[1])],
        out_specs=pl.BlockSpec((tm, tn), lambda j, i: (i, j)),
        out_shape=jax.ShapeDtypeStruct((rows, D_MODEL), BF16),
        scratch_shapes=[pltpu.VMEM((ret.shape[1], tn), BF16), pltpu.VMEM((gm.shape[1], tn), BF16),
                        pltpu.VMEM((na.shape[1], tn), BF16)],
        compiler_params=_cparams(2),
    )(ret, gm, na, gates, gates, gates, w_br, w_bg, w_bn)


def kernel(x_prompt, x_sample, cache_k, cache_v, state_ret, c, c_ctx, w_ada, b_ada, norm_pre, norm_post, ffn_w_in, ffn_w_out, w_in, ret_decay_logit, gm_norm, gm_ws, gm_bs, na_rpb, w_branch_ret, w_branch_gm, w_branch_na, w_out):
    n_ctx, ctx_seq, d = x_prompt.shape
    n_lat, lat_seq, _ = x_sample.shape
    depth = w_ada.shape[0]
    n_ctx_rows = n_ctx * ctx_seq
    dims = (n_ctx_rows, lat_seq)

    x = jnp.concatenate([x_prompt.reshape(n_ctx_rows, d), x_sample.reshape(n_lat * lat_seq, d)], axis=0)
    cond = jnp.concatenate([c_ctx[None, :], c, jnp.zeros((MOD_ROWS - 1 - n_lat, d), F32)], axis=0)
    mod = _adaln(cond, w_ada, b_ada).reshape(depth, MOD_ROWS * 3 * N_SUB, 1, d)
    rot = _rotary_tables(lat_seq)

    h = _prenorm(x, norm_pre[0, 0], mod[0], dims)
    new_k, new_v, new_s = [], [], []
    for l in range(depth):
        up = _ffn_up(h, ffn_w_in[l, 0])
        x, h = _rowout(up, ffn_w_out[l, 0], x, mod[l], 0, 0.5, norm_post[l, 0], dims, (norm_pre[l, 1], mod[l], 1))

        z = _proj(h, w_in[l], T_QA, T_KC - T_QA, BF16)
        kv = _proj(h, w_in[l], T_KC, T_GATES - T_KC, F32)
        gates = _proj(h, w_in[l], T_GATES, T_END - T_GATES, BF16)

        logits = jnp.broadcast_to(ret_decay_logit[l].T[:, :, None, None], (RET_HEADS, 2, 1, RET_DV))
        ret, s_ctx = _retention_ctx(z, logits, n_ctx, ctx_seq)
        ret = _retention_latent(z, logits, rot, state_ret, l, ret, n_lat, lat_seq, n_ctx_rows)
        gm = _gmlp(z, gm_norm[l], gm_ws[l], gm_bs[l])
        na = _attn_ctx(z, kv, n_ctx, ctx_seq)
        na = _attn_latent(z, kv, cache_k, cache_v, _window_bias(na_rpb[l], lat_seq), l, na, n_lat, lat_seq, n_ctx_rows)
        m = _merge(ret, gm, na, gates, w_branch_ret[l], w_branch_gm[l], w_branch_na[l])
        x, h = _rowout(m, w_out[l], x, mod[l], 1, 1.0, norm_post[l, 1], dims, (norm_pre[l, 2], mod[l], 2))

        up = _ffn_up(h, ffn_w_in[l, 1])
        nxt = (norm_pre[l + 1, 0], mod[l + 1], 0) if l + 1 < depth else None
        x, h = _rowout(up, ffn_w_out[l, 1], x, mod[l], 2, 0.5, norm_post[l, 2], dims, nxt)

        new_k.append(kv[:n_ctx_rows, :NA_W].reshape(n_ctx, ctx_seq, NA_HEADS, NA_HEAD_DIM))
        new_v.append(kv[:n_ctx_rows, NA_W:].reshape(n_ctx, ctx_seq, NA_HEADS, NA_HEAD_DIM))
        new_s.append(s_ctx)

    y_prompt = x[:n_ctx_rows].reshape(n_ctx, ctx_seq, d)
    y_sample = x[n_ctx_rows:].reshape(n_lat, lat_seq, d)
    return (y_prompt, y_sample, jnp.stack(new_k, axis=1), jnp.stack(new_v, axis=1), jnp.stack(new_s, axis=1))
```

```python
import functools

import jax
import jax.numpy as jnp
from jax import lax
from jax.experimental import pallas as pl
from jax.experimental.pallas import tpu as pltpu

D_MODEL = 2048
GRID_W = 64
RET_HEADS = 8
RET_DK = 128
RET_DV = 256
CHUNK = 128
GM_GROUPS = 4
GM_WIDTH = 1024
GM_GROUP_DIM = GM_WIDTH // GM_GROUPS
NA_HEADS = 8
NA_HEAD_DIM = 128
NA_WIN_ROWS = 8
NA_WIN_COLS = 16
N_SUB = 3
ROPE_BASE = 10000.0
NORM_EPS = 1e-6
NEG_INF = -1e30
NA_W = NA_HEADS * NA_HEAD_DIM
RPB_ROWS = 2 * NA_WIN_ROWS - 1
RPB_COLS = 2 * NA_WIN_COLS - 1

BF16 = jnp.bfloat16
F32 = jnp.float32

V7X_VMEM_LIMIT_BYTES = 56 * 1024 * 1024
ROW_TILE = 512
PROJ_COL_TILE = 1024
MOD_ROWS = 16
T_QA, T_KA, T_VA, T_GA, T_UB, T_VB, T_QC, T_KC, T_VC, T_GATES, T_END = 0, 1, 2, 4, 6, 7, 8, 9, 10, 11, 17


def _cparams(n_axes):
    return pltpu.CompilerParams(dimension_semantics=("arbitrary",) * n_axes,
                                vmem_limit_bytes=V7X_VMEM_LIMIT_BYTES)


def _silu(x):
    return x * jax.nn.sigmoid(x)


def _dot(a, b):
    return jnp.dot(a, b, preferred_element_type=F32)


def _dot_nt(a, b):
    return lax.dot_general(a, b, (((1,), (1,)), ((), ())), preferred_element_type=F32)


def _dot_tn(a, b):
    return lax.dot_general(a, b, (((0,), (0,)), ((), ())), preferred_element_type=F32)


def _rms(x):
    return x * lax.rsqrt(jnp.mean(x * x, axis=-1, keepdims=True) + NORM_EPS)


def _modulated_input(x, g_pre, scale, shift):
    return (_rms(x) * g_pre) * (1.0 + scale) + shift


def _adaln_kernel(c_ref, w_ref, b_ref, o_ref):
    s = _silu(c_ref[...]).astype(BF16)
    o_ref[...] = _dot(s, w_ref[...].astype(BF16)) + b_ref[...]


def _adaln(cond, w_ada, b_ada):
    depth, d, n = w_ada.shape
    tn = 1024
    return pl.pallas_call(
        _adaln_kernel,
        grid=(depth, n // tn),
        in_specs=[pl.BlockSpec((MOD_ROWS, d), lambda l, j: (0, 0)),
                  pl.BlockSpec((None, d, tn), lambda l, j: (l, 0, j)),
                  pl.BlockSpec((None, 1, tn), lambda l, j: (l, 0, j))],
        out_specs=pl.BlockSpec((None, MOD_ROWS, tn), lambda l, j: (l, 0, j)),
        out_shape=jax.ShapeDtypeStruct((depth, MOD_ROWS, n), F32),
        compiler_params=_cparams(2),
    )(cond, w_ada, b_ada.reshape(depth, 1, n))


def _mod_spec(dims, tm, sub, part, tile_of_step=lambda s: s):
    n_ctx_rows, lat_seq = dims
    nc = n_ctx_rows // tm
    per = lat_seq // tm

    def index(s, *_):
        i = tile_of_step(s)
        row = jnp.where(i < nc, 0, 1 + (i - nc) // per)
        return (row * (3 * N_SUB) + 3 * sub + part, 0, 0)

    return pl.BlockSpec((None, 1, D_MODEL), index)


def _vec_spec():
    return pl.BlockSpec((1, D_MODEL), lambda *_: (0, 0))


def _split_row_specs(dims, tm, tile_of_step=lambda s: s):
    nc = dims[0] // tm
    ctx = pl.BlockSpec((tm, D_MODEL), lambda s, *_: (jnp.minimum(tile_of_step(s), nc - 1), 0))
    lat = pl.BlockSpec((tm, D_MODEL), lambda s, *_: (jnp.maximum(tile_of_step(s) - nc, 0), 0))
    return ctx, lat


def _prenorm_kernel(xc_ref, xl_ref, g_ref, sc_ref, sh_ref, h_ref, *, nc):
    x = jnp.where(pl.program_id(0) < nc, xc_ref[...], xl_ref[...])
    h_ref[...] = _modulated_input(x, g_ref[...], sc_ref[...], sh_ref[...]).astype(BF16)


def _prenorm(x_ctx, x_lat, g_pre, mod_tab, dims):
    rows = x_ctx.shape[0] + x_lat.shape[0]
    tm = ROW_TILE
    ctx_spec, lat_spec = _split_row_specs(dims, tm)
    return pl.pallas_call(
        functools.partial(_prenorm_kernel, nc=dims[0] // tm),
        grid=(rows // tm,),
        in_specs=[ctx_spec, lat_spec, _vec_spec(), _mod_spec(dims, tm, 0, 1), _mod_spec(dims, tm, 0, 0)],
        out_specs=pl.BlockSpec((tm, D_MODEL), lambda i: (i, 0)),
        out_shape=jax.ShapeDtypeStruct((rows, D_MODEL), BF16),
        compiler_params=_cparams(1),
    )(x_ctx, x_lat, g_pre.reshape(1, D_MODEL), mod_tab, mod_tab)


def _ffn_up_kernel(h_ref, wa_ref, wb_ref, o_ref, wa_s, wb_s):
    @pl.when(pl.program_id(1) == 0)
    def _():
        wa_s[...] = wa_ref[...].astype(BF16)
        wb_s[...] = wb_ref[...].astype(BF16)

    h = h_ref[...]
    a = _dot(h, wa_s[...])
    b = _dot(h, wb_s[...])
    o_ref[...] = (_silu(a) * b).astype(BF16)


def _ffn_up(h, ffn_w_in, layer, which):
    rows, d = h.shape
    f = ffn_w_in.shape[-1] // 2
    tm, tf = ROW_TILE, 512
    nf = f // tf
    return pl.pallas_call(
        _ffn_up_kernel,
        grid=(nf, rows // tm),
        in_specs=[pl.BlockSpec((tm, d), lambda j, i: (i, 0)),
                  pl.BlockSpec((None, None, d, tf), lambda j, i: (layer, which, 0, j)),
                  pl.BlockSpec((None, None, d, tf), lambda j, i: (layer, which, 0, nf + j))],
        out_specs=pl.BlockSpec((tm, tf), lambda j, i: (i, j)),
        out_shape=jax.ShapeDtypeStruct((rows, f), BF16),
        scratch_shapes=[pltpu.VMEM((d, tf), BF16), pltpu.VMEM((d, tf), BF16)],
        compiler_params=_cparams(2),
    )(h, ffn_w_in, ffn_w_in)


def _rowout_kernel(*refs, res_w, n_load, tk, nc, split_in, split_out, with_next):
    refs = list(refs)
    a_ref, w_ref = refs[:2]
    del refs[:2]
    x_refs = [refs.pop(0) for _ in range(2 if split_in else 1)]
    gpost_ref, gate_ref = refs[:2]
    del refs[:2]
    if with_next:
        gpre_ref, sc_ref, sh_ref = refs[:3]
        del refs[:3]
    xo_refs = [refs.pop(0) for _ in range(2 if split_out else 1)]
    ho_ref = refs.pop(0) if with_next else None
    (w_s,) = refs
    s = pl.program_id(0)

    @pl.when(s < n_load)
    def _():
        w_s[pl.ds(pl.multiple_of(s * tk, tk), tk), :] = w_ref[...].astype(BF16)

    @pl.when(s >= n_load)
    def _():
        i = s - n_load
        y = _dot(a_ref[...], w_s[...])
        x = jnp.where(i < nc, x_refs[0][...], x_refs[1][...]) if split_in else x_refs[0][...]
        xn = x + (res_w * gate_ref[...]) * (_rms(y) * gpost_ref[...])
        if split_out:
            @pl.when(i < nc)
            def _():
                xo_refs[0][...] = xn

            @pl.when(i >= nc)
            def _():
                xo_refs[1][...] = xn
        else:
            xo_refs[0][...] = xn
        if with_next:
            ho_ref[...] = _modulated_input(xn, gpre_ref[...], sc_ref[...], sh_ref[...]).astype(BF16)


def _rowout(a, w_all, w_index, x, mod_tab, sub, res_w, g_post, dims, nxt=None, split_out=False):
    rows, kdim = a.shape
    tm = ROW_TILE if kdim <= D_MODEL else ROW_TILE // 2
    tk = 256
    n_load = kdim // tk
    nc = dims[0] // tm
    split_in = isinstance(x, tuple)

    def tile(s):
        return jnp.maximum(s - n_load, 0)

    row_spec = pl.BlockSpec((tm, D_MODEL), lambda s: (tile(s), 0))
    lead = (None,) * len(w_index)
    in_specs = [pl.BlockSpec((tm, kdim), lambda s: (tile(s), 0)),
                pl.BlockSpec(lead + (tk, D_MODEL), lambda s: tuple(w_index) + (jnp.minimum(s, n_load - 1), 0))]
    args = [a, w_all]
    if split_in:
        in_specs += list(_split_row_specs(dims, tm, tile))
        args += list(x)
    else:
        in_specs.append(row_spec)
        args.append(x)
    in_specs += [_vec_spec(), _mod_spec(dims, tm, sub, 2, tile)]
    args += [g_post.reshape(1, D_MODEL), mod_tab]
    if nxt is not None:
        g_pre, mod_next, sub_next = nxt
        in_specs += [_vec_spec(), _mod_spec(dims, tm, sub_next, 1, tile), _mod_spec(dims, tm, sub_next, 0, tile)]
        args += [g_pre.reshape(1, D_MODEL), mod_next, mod_next]
    if split_out:
        out_specs = list(_split_row_specs(dims, tm, tile))
        out_shape = [jax.ShapeDtypeStruct((dims[0], D_MODEL), F32),
                     jax.ShapeDtypeStruct((rows - dims[0], D_MODEL), F32)]
    else:
        out_specs = [row_spec]
        out_shape = [jax.ShapeDtypeStruct((rows, D_MODEL), F32)]
    if nxt is not None:
        out_specs.append(row_spec)
        out_shape.append(jax.ShapeDtypeStruct((rows, D_MODEL), BF16))
    out = pl.pallas_call(
        functools.partial(_rowout_kernel, res_w=res_w, n_load=n_load, tk=tk, nc=nc, split_in=split_in,
                          split_out=split_out, with_next=nxt is not None),
        grid=(n_load + rows // tm,),
        in_specs=in_specs,
        out_specs=out_specs,
        out_shape=out_shape,
        scratch_shapes=[pltpu.VMEM((kdim, D_MODEL), BF16)],
        compiler_params=_cparams(1),
    )(*args)
    n_x = 2 if split_out else 1
    x_new = tuple(out[:n_x]) if split_out else out[0]
    return x_new, (out[n_x] if nxt is not None else None)


def _proj_kernel(*refs):
    h_ref, w_ref = refs[:2]
    o_ref, w_s = refs[-2:]

    @pl.when(pl.program_id(1) == 0)
    def _():
        w_s[...] = w_ref[...].astype(BF16)

    o_ref[...] = _dot(h_ref[...], w_s[...]).astype(o_ref.dtype).reshape(o_ref.shape)


def _proj(h, w_in, layer, tile0, n_tiles, out_dtype, row_tile0=0, n_row_tiles=None):
    d = h.shape[1]
    tm, tn = ROW_TILE, PROJ_COL_TILE
    n_row_tiles = h.shape[0] // tm if n_row_tiles is None else n_row_tiles
    return pl.pallas_call(
        _proj_kernel,
        grid=(n_tiles, n_row_tiles),
        in_specs=[pl.BlockSpec((tm, d), lambda j, i: (row_tile0 + i, 0)),
                  pl.BlockSpec((None, d, tn), lambda j, i: (layer, 0, tile0 + j))],
        out_specs=pl.BlockSpec((tm, tn), lambda j, i: (i, j)),
        out_shape=jax.ShapeDtypeStruct((n_row_tiles * tm, n_tiles * tn), out_dtype),
        scratch_shapes=[pltpu.VMEM((d, tn), BF16)],
        compiler_params=_cparams(2),
    )(h, w_in)


def _proj_cache(h, w_in, layer, tile, n_seq, seq, depth, prev):
    d = h.shape[1]
    tm, tn = ROW_TILE, PROJ_COL_TILE
    per = tm // seq
    in_specs = [pl.BlockSpec((tm, d), lambda j, i: (i, 0)),
                pl.BlockSpec((None, d, tn), lambda j, i: (layer, 0, tile))]
    args = [h, w_in]
    aliases = {}
    if prev is not None:
        in_specs.append(pl.BlockSpec(memory_space=pl.ANY))
        args.append(prev)
        aliases = {2: 0}
    return pl.pallas_call(
        _proj_kernel,
        grid=(1, n_seq // per),
        in_specs=in_specs,
        out_specs=pl.BlockSpec((per, None, seq, tn), lambda j, i: (i, layer, 0, 0)),
        out_shape=jax.ShapeDtypeStruct((n_seq, depth, seq, tn), F32),
        scratch_shapes=[pltpu.VMEM((d, tn), BF16)],
        input_output_aliases=aliases,
        compiler_params=_cparams(2),
    )(*args)


def _log_sigmoid(x):
    return jnp.minimum(x, 0.0) - jnp.log1p(jnp.exp(-jnp.abs(x)))


def _retention_kernel(*refs, n_chunks, latent):
    if latent:
        (logit_ref, q_ref, k_ref, v_ref, g_ref, cos_ref, sa_ref, sb_ref, s0_ref, _,
         o_ref, q_s, k_s, kf_s, kb_s, sf_s, sb_s) = refs
    else:
        (logit_ref, q_ref, k_ref, v_ref, g_ref) = refs[:5]
        (o_ref, so_ref, q_s, k_s, kf_s, kb_s, sf_s, sb_s) = refs[-8:]
    c = CHUNK
    lg = _log_sigmoid(logit_ref[...])
    lgf, lgb = lg[0], lg[1]
    lgf_k, lgb_k = lgf[:, :RET_DK], lgb[:, :RET_DK]

    ii = lax.broadcasted_iota(jnp.int32, (c, c), 0)
    jj = lax.broadcasted_iota(jnp.int32, (c, c), 1)
    dist = (ii - jj).astype(F32)
    dmask = (jnp.where(dist >= 0, jnp.exp(lgf_k * jnp.maximum(dist, 0.0)), 0.0)
             + jnp.where(dist <= 0, jnp.exp(lgb_k * jnp.maximum(-dist, 0.0)), 0.0))
    pos_k = ii.astype(F32)
    zeta_f = jnp.exp(lgf_k * (c - 1.0 - pos_k))
    zeta_b = jnp.exp(lgb_k * pos_k)
    pos_v = lax.broadcasted_iota(jnp.int32, (c, RET_DV), 0).astype(F32)
    xi_f = jnp.exp(lgf * (pos_v + 1.0))
    xi_b = jnp.exp(lgb * (c - pos_v))
    decay_f = jnp.exp(lgf * c)
    decay_b = jnp.exp(lgb * c)

    for n in range(n_chunks):
        sl = slice(n * c, (n + 1) * c)
        q = q_ref[sl, :].astype(F32)
        k = k_ref[sl, :].astype(F32)
        if latent:
            cos, sa, sb = cos_ref[sl, :], sa_ref[sl, :], sb_ref[sl, :]
            q = q * cos + pltpu.roll(q, 96, 1) * sa + pltpu.roll(q, 32, 1) * sb
            k = k * cos + pltpu.roll(k, 96, 1) * sa + pltpu.roll(k, 32, 1) * sb
        k = k * (RET_DK ** -0.5)
        q_s[sl, :] = q.astype(BF16)
        k_s[sl, :] = k.astype(BF16)
        kf_s[sl, :] = (k * zeta_f).astype(BF16)
        kb_s[sl, :] = (k * zeta_b).astype(BF16)

    s = s0_ref[0] if latent else jnp.zeros((RET_DK, RET_DV), F32)
    for n in range(n_chunks):
        sl = slice(n * c, (n + 1) * c)
        sf_s[n] = s.astype(BF16)
        s = decay_f * s + _dot_tn(kf_s[sl, :], v_ref[sl, :])
    if not latent:
        so_ref[0] = s
    s = s0_ref[1] if latent else jnp.zeros((RET_DK, RET_DV), F32)
    for n in reversed(range(n_chunks)):
        sl = slice(n * c, (n + 1) * c)
        sb_s[n] = s.astype(BF16)
        s = decay_b * s + _dot_tn(kb_s[sl, :], v_ref[sl, :])
    if not latent:
        so_ref[1] = s

    for n in range(n_chunks):
        sl = slice(n * c, (n + 1) * c)
        q = q_s[sl, :]
        scores = (_dot_nt(q, k_s[sl, :]) * dmask).astype(BF16)
        o = (_dot(scores, v_ref[sl, :])
             + _dot(q, sf_s[n]) * xi_f + _dot(q, sb_s[n]) * xi_b)
        o = _rms(o) * _silu(g_ref[sl, :].astype(F32))
        o_ref[sl, :] = o.astype(BF16)


def _retention_scratch(seq):
    n_chunks = seq // CHUNK
    return [pltpu.VMEM((seq, RET_DK), BF16)] * 4 + [pltpu.VMEM((n_chunks, RET_DK, RET_DV), BF16)] * 2


def _retention_in_specs(seq, r0):
    kq, kk = T_QA * PROJ_COL_TILE // RET_DK, T_KA * PROJ_COL_TILE // RET_DK
    kv, kg = T_VA * PROJ_COL_TILE // RET_DV, T_GA * PROJ_COL_TILE // RET_DV
    return [pl.BlockSpec((None, 2, 1, RET_DV), lambda b, h: (h, 0, 0, 0)),
            pl.BlockSpec((seq, RET_DK), lambda b, h: (r0 + b, kq + h)),
            pl.BlockSpec((seq, RET_DK), lambda b, h: (r0 + b, kk + h)),
            pl.BlockSpec((seq, RET_DV), lambda b, h: (r0 + b, kv + h)),
            pl.BlockSpec((seq, RET_DV), lambda b, h: (r0 + b, kg + h))]


def _retention_ctx(z, logits, n_seq, seq, layer, depth, prev_states):
    rows = z.shape[0]
    in_specs = _retention_in_specs(seq, 0)
    args = [logits, z, z, z, z]
    aliases = {}
    if prev_states is not None:
        in_specs.append(pl.BlockSpec(memory_space=pl.ANY))
        args.append(prev_states)
        aliases = {5: 1}
    return pl.pallas_call(
        functools.partial(_retention_kernel, n_chunks=seq // CHUNK, latent=False),
        grid=(n_seq, RET_HEADS),
        in_specs=in_specs,
        out_specs=[pl.BlockSpec((seq, RET_DV), lambda b, h: (b, h)),
                   pl.BlockSpec((None, None, 2, None, RET_DK, RET_DV), lambda b, h: (b, layer, 0, h, 0, 0))],
        out_shape=[jax.ShapeDtypeStruct((rows, RET_HEADS * RET_DV), BF16),
                   jax.ShapeDtypeStruct((n_seq, depth, 2, RET_HEADS, RET_DK, RET_DV), F32)],
        scratch_shapes=_retention_scratch(seq),
        input_output_aliases=aliases,
        compiler_params=_cparams(2),
    )(*args)


def _retention_latent(z, logits, rot, state, layer, ret_all, n_seq, seq, row0):
    r0 = row0 // seq
    tab_spec = pl.BlockSpec((seq, RET_DK), lambda b, h: (0, 0))
    return pl.pallas_call(
        functools.partial(_retention_kernel, n_chunks=seq // CHUNK, latent=True),
        grid=(n_seq, RET_HEADS),
        in_specs=_retention_in_specs(seq, r0) + [
            tab_spec, tab_spec, tab_spec,
            pl.BlockSpec((None, None, 2, None, RET_DK, RET_DV), lambda b, h: (b, layer, 0, h, 0, 0)),
            pl.BlockSpec(memory_space=pl.ANY)],
        out_specs=pl.BlockSpec((seq, RET_DV), lambda b, h: (r0 + b, h)),
        out_shape=jax.ShapeDtypeStruct(ret_all.shape, BF16),
        scratch_shapes=_retention_scratch(seq),
        input_output_aliases={9: 0},
        compiler_params=_cparams(2),
    )(logits, z, z, z, z, rot[0], rot[1], rot[2], state, ret_all)


def _gmlp_kernel(u_ref, v_ref, gn_ref, ws_ref, bs_ref, o_ref, *, n_chunks):
    v = v_ref[...].astype(F32)
    mu = jnp.mean(v, axis=-1, keepdims=True)
    vc = v - mu
    var = jnp.mean(vc * vc, axis=-1, keepdims=True)
    vn = (vc * lax.rsqrt(var + NORM_EPS) * gn_ref[...]).astype(BF16)
    for g in range(GM_GROUPS):
        w = ws_ref[g].astype(BF16)
        cols = slice(g * GM_GROUP_DIM, (g + 1) * GM_GROUP_DIM)
        for n in range(n_chunks):
            rows = slice(n * CHUNK, (n + 1) * CHUNK)
            mixed = _dot(w, vn[rows, cols]) + bs_ref[g]
            o_ref[rows, cols] = (u_ref[rows, cols].astype(F32) * mixed).astype(BF16)


def _gmlp(z, gm_norm, gm_ws, gm_bs):
    rows = z.shape[0]
    tm = ROW_TILE
    return pl.pallas_call(
        functools.partial(_gmlp_kernel, n_chunks=tm // CHUNK),
        grid=(rows // tm,),
        in_specs=[pl.BlockSpec((tm, GM_WIDTH), lambda i: (i, T_UB * PROJ_COL_TILE // GM_WIDTH)),
                  pl.BlockSpec((tm, GM_WIDTH), lambda i: (i, T_VB * PROJ_COL_TILE // GM_WIDTH)),
                  pl.BlockSpec((1, GM_WIDTH), lambda i: (0, 0)),
                  pl.BlockSpec((GM_GROUPS, CHUNK, CHUNK), lambda i: (0, 0, 0)),
                  pl.BlockSpec((GM_GROUPS, CHUNK, 1), lambda i: (0, 0, 0))],
        out_specs=pl.BlockSpec((tm, GM_WIDTH), lambda i: (i, 0)),
        out_shape=jax.ShapeDtypeStruct((rows, GM_WIDTH), BF16),
        compiler_params=_cparams(1),
    )(z, z, gm_norm.reshape(1, GM_WIDTH), gm_ws, gm_bs.reshape(GM_GROUPS, CHUNK, 1))


def _softmax_rows(parts):
    m = functools.reduce(jnp.maximum, [jnp.max(s, axis=-1, keepdims=True) for s in parts])
    e = [jnp.exp(s - m) for s in parts]
    denom = functools.reduce(lambda a, b: a + b, [jnp.sum(x, axis=-1, keepdims=True) for x in e])
    return [(x / denom).astype(BF16) for x in e]


def _attn_ctx_kernel(q_ref, k_ref, v_ref, o_ref):
    scale = NA_HEAD_DIM ** -0.5
    for h in range(NA_HEADS):
        cols = slice(h * NA_HEAD_DIM, (h + 1) * NA_HEAD_DIM)
        (p,) = _softmax_rows([_dot_nt(q_ref[:, cols], k_ref[:, cols].astype(BF16)) * scale])
        o_ref[:, cols] = _dot(p, v_ref[:, cols].astype(BF16)).astype(BF16)


def _attn_ctx(z, new_k, new_v, layer, n_seq, seq):
    rows = z.shape[0]
    cache_spec = pl.BlockSpec((None, None, seq, NA_W), lambda b: (b, layer, 0, 0))
    return pl.pallas_call(
        _attn_ctx_kernel,
        grid=(n_seq,),
        in_specs=[pl.BlockSpec((seq, NA_W), lambda b: (b, T_QC * PROJ_COL_TILE // NA_W)), cache_spec, cache_spec],
        out_specs=pl.BlockSpec((seq, NA_W), lambda b: (b, 0)),
        out_shape=jax.ShapeDtypeStruct((rows, NA_W), BF16),
        compiler_params=_cparams(1),
    )(z, new_k, new_v)


def _window_rows(seq):
    rows = seq // GRID_W
    wr = min(NA_WIN_ROWS, rows)
    return rows, wr, [min(max(r - wr // 2, 0), rows - wr) for r in range(rows)]


def _build_window_bias(rpb_ref, base, bias_s, seq):
    rows, wr, row_start = _window_rows(seq)
    shape = (GRID_W, 2 * GRID_W)
    qc = lax.broadcasted_iota(jnp.int32, shape, 0)
    lane = lax.broadcasted_iota(jnp.int32, shape, 1)
    upper = lane >= GRID_W
    kc = jnp.where(upper, lane - GRID_W, lane)
    col_start = jnp.clip(qc - NA_WIN_COLS // 2, 0, GRID_W - NA_WIN_COLS)
    col_ok = (kc >= col_start) & (kc < col_start + NA_WIN_COLS)
    col_off = jnp.clip(kc - qc + NA_WIN_COLS - 1, 0, RPB_COLS - 1)
    tabs = [jnp.zeros(shape, F32)] * RPB_ROWS
    for c in range(RPB_COLS):
        hit = col_off == c
        tabs = [jnp.where(hit, rpb_ref[base + dr * RPB_COLS + c], t) for dr, t in enumerate(tabs)]
    neg = jnp.full(shape, NEG_INF, F32)
    tabs = [jnp.where(col_ok, t, neg) for t in tabs]
    for qr in range(rows):
        for m in range(rows // 2):
            halves = []
            for kr in (2 * m, 2 * m + 1):
                inside = row_start[qr] <= kr < row_start[qr] + wr
                halves.append(tabs[kr - qr + NA_WIN_ROWS - 1] if inside else neg)
            blk = halves[0] if halves[0] is halves[1] else jnp.where(upper, halves[1], halves[0])
            bias_s[qr * GRID_W:(qr + 1) * GRID_W, m * 2 * GRID_W:(m + 1) * 2 * GRID_W] = blk


def _attn_lat_kernel(rpb_ref, q_ref, k_ref, v_ref, kc_ref, vc_ref, _, o_ref, bias_s, *, q_block):
    seq = q_ref.shape[0]

    @pl.when(pl.program_id(1) == 0)
    def _():
        _build_window_bias(rpb_ref, pl.program_id(0) * (RPB_ROWS * RPB_COLS), bias_s, seq)

    scale = NA_HEAD_DIM ** -0.5
    kc = kc_ref[...].astype(BF16)
    vc = vc_ref[...].astype(BF16)
    _, wr, row_start = _window_rows(seq)
    rows_per_block = q_block // GRID_W
    for n in range(seq // q_block):
        lo = row_start[n * rows_per_block] * GRID_W
        hi = (row_start[(n + 1) * rows_per_block - 1] + wr) * GRID_W
        rows = slice(n * q_block, (n + 1) * q_block)
        q = q_ref[rows, :]
        s_loc = _dot_nt(q, k_ref[lo:hi, :]) * scale + bias_s[rows, lo:hi]
        s_ctx = _dot_nt(q, kc) * scale
        p_loc, p_ctx = _softmax_rows([s_loc, s_ctx])
        o_ref[rows, :] = (_dot(p_loc, v_ref[lo:hi, :]) + _dot(p_ctx, vc)).astype(BF16)


def _attn_latent(z, kv_lat, cache_k, cache_v, rpb, layer, na_all, n_seq, seq, row0):
    r0 = row0 // seq
    past = cache_k.shape[2]
    kq = T_QC * PROJ_COL_TILE // NA_HEAD_DIM
    cache_spec = pl.BlockSpec((None, None, past, NA_HEAD_DIM), lambda h, b: (b, layer, 0, h))
    return pl.pallas_call(
        functools.partial(_attn_lat_kernel, q_block=256),
        grid=(NA_HEADS, n_seq),
        in_specs=[pl.BlockSpec(memory_space=pltpu.SMEM),
                  pl.BlockSpec((seq, NA_HEAD_DIM), lambda h, b: (r0 + b, kq + h)),
                  pl.BlockSpec((seq, NA_HEAD_DIM), lambda h, b: (b, h)),
                  pl.BlockSpec((seq, NA_HEAD_DIM), lambda h, b: (b, NA_HEADS + h)),
                  cache_spec, cache_spec,
                  pl.BlockSpec(memory_space=pl.ANY)],
        out_specs=pl.BlockSpec((seq, NA_HEAD_DIM), lambda h, b: (r0 + b, h)),
        out_shape=jax.ShapeDtypeStruct(na_all.shape, BF16),
        scratch_shapes=[pltpu.VMEM((seq, seq), F32)],
        input_output_aliases={6: 0},
        compiler_params=_cparams(2),
    )(rpb.reshape(-1), z, kv_lat, kv_lat,
      cache_k.reshape(cache_k.shape[:3] + (NA_W,)), cache_v.reshape(cache_v.shape[:3] + (NA_W,)), na_all)


def _rotary_tables(seq):
    t = jnp.arange(seq)
    half = RET_DK // 2
    inv = ROPE_BASE ** (-jnp.arange(0, half, 2, dtype=F32) / half)
    ang_r = (t // GRID_W).astype(F32)[:, None] * inv[None, :]
    ang_c = (t % GRID_W).astype(F32)[:, None] * inv[None, :]
    zero = jnp.zeros_like(ang_r)
    cos = jnp.concatenate([jnp.cos(ang_r), jnp.cos(ang_r), jnp.cos(ang_c), jnp.cos(ang_c)], axis=-1)
    sa = jnp.concatenate([-jnp.sin(ang_r), zero, -jnp.sin(ang_c), zero], axis=-1)
    sb = jnp.concatenate([zero, jnp.sin(ang_r), zero, jnp.sin(ang_c)], axis=-1)
    return cos, sa, sb


def _merge_kernel(ret_ref, gm_ref, na_ref, ga_ref, gb_ref, gc_ref, wr_ref, wg_ref, wn_ref, o_ref, wr_s, wg_s, wn_s):
    @pl.when(pl.program_id(1) == 0)
    def _():
        wr_s[...] = wr_ref[...].astype(BF16)
        wg_s[...] = wg_ref[...].astype(BF16)
        wn_s[...] = wn_ref[...].astype(BF16)

    m = (jax.nn.sigmoid(ga_ref[...].astype(F32)) * _dot(ret_ref[...], wr_s[...])
         + jax.nn.sigmoid(gb_ref[...].astype(F32)) * _dot(gm_ref[...], wg_s[...])
         + jax.nn.sigmoid(gc_ref[...].astype(F32)) * _dot(na_ref[...], wn_s[...]))
    o_ref[...] = m.astype(BF16)


def _merge(ret, gm, na, gates, w_br, w_bg, w_bn, layer):
    rows = ret.shape[0]
    tm, tn = ROW_TILE, 512
    nj = D_MODEL // tn

    def a_spec(width):
        return pl.BlockSpec((tm, width), lambda j, i: (i, 0))

    def gate_spec(which):
        return pl.BlockSpec((tm, tn), lambda j, i: (i, which * nj + j))

    def w_spec(width):
        return pl.BlockSpec((None, width, tn), lambda j, i: (layer, 0, j))

    return pl.pallas_call(
        _merge_kernel,
        grid=(nj, rows // tm),
        in_specs=[a_spec(ret.shape[1]), a_spec(gm.shape[1]), a_spec(na.shape[1]),
                  gate_spec(0), gate_spec(1), gate_spec(2),
                  w_spec(ret.shape[1]), w_spec(gm.shape[1]), w_spec(na.shape[1])],
        out_specs=pl.BlockSpec((tm, tn), lambda j, i: (i, j)),
        out_shape=jax.ShapeDtypeStruct((rows, D_MODEL), BF16),
        scratch_shapes=[pltpu.VMEM((ret.shape[1], tn), BF16), pltpu.VMEM((gm.shape[1], tn), BF16),
                        pltpu.VMEM((na.shape[1], tn), BF16)],
        compiler_params=_cparams(2),
    )(ret, gm, na, gates, gates, gates, w_br, w_bg, w_bn)


def kernel(x_prompt, x_sample, cache_k, cache_v, state_ret, c, c_ctx, w_ada, b_ada, norm_pre, norm_post, ffn_w_in, ffn_w_out, w_in, ret_decay_logit, gm_norm, gm_ws, gm_bs, na_rpb, w_branch_ret, w_branch_gm, w_branch_na, w_out):
    n_ctx, ctx_seq, d = x_prompt.shape
    n_lat, lat_seq, _ = x_sample.shape
    depth = w_ada.shape[0]
    n_ctx_rows = n_ctx * ctx_seq
    dims = (n_ctx_rows, lat_seq)
    lat_tile0 = n_ctx_rows // ROW_TILE
    n_lat_tiles = n_lat * lat_seq // ROW_TILE

    x = (x_prompt.reshape(n_ctx_rows, d), x_sample.reshape(n_lat * lat_seq, d))
    cond = jnp.concatenate([c_ctx[None, :], c, jnp.zeros((MOD_ROWS - 1 - n_lat, d), F32)], axis=0)
    mod = _adaln(cond, w_ada, b_ada).reshape(depth, MOD_ROWS * 3 * N_SUB, 1, d)
    rot = _rotary_tables(lat_seq)

    h = _prenorm(x[0], x[1], norm_pre[0, 0], mod[0], dims)
    new_k = new_v = new_s = None
    for l in range(depth):
        up = _ffn_up(h, ffn_w_in, l, 0)
        x, h = _rowout(up, ffn_w_out, (l, 0), x, mod[l], 0, 0.5, norm_post[l, 0], dims, (norm_pre[l, 1], mod[l], 1))

        z = _proj(h, w_in, l, T_QA, T_KC - T_QA, BF16)
        new_k = _proj_cache(h, w_in, l, T_KC, n_ctx, ctx_seq, depth, new_k)
        new_v = _proj_cache(h, w_in, l, T_VC, n_ctx, ctx_seq, depth, new_v)
        kv_lat = _proj(h, w_in, l, T_KC, T_GATES - T_KC, BF16, lat_tile0, n_lat_tiles)
        gates = _proj(h, w_in, l, T_GATES, T_END - T_GATES, BF16)

        logits = jnp.broadcast_to(ret_decay_logit[l].T[:, :, None, None], (RET_HEADS, 2, 1, RET_DV))
        ret, new_s = _retention_ctx(z, logits, n_ctx, ctx_seq, l, depth, new_s)
        ret = _retention_latent(z, logits, rot, state_ret, l, ret, n_lat, lat_seq, n_ctx_rows)
        gm = _gmlp(z, gm_norm[l], gm_ws[l], gm_bs[l])
        na = _attn_ctx(z, new_k, new_v, l, n_ctx, ctx_seq)
        na = _attn_latent(z, kv_lat, cache_k, cache_v, na_rpb[l], l, na, n_lat, lat_seq, n_ctx_rows)
        m = _merge(ret, gm, na, gates, w_branch_ret, w_branch_gm, w_branch_na, l)
        x, h = _rowout(m, w_out, (l,), x, mod[l], 1, 1.0, norm_post[l, 1], dims, (norm_pre[l, 2], mod[l], 2))

        up = _ffn_up(h, ffn_w_in, l, 1)
        last = l + 1 == depth
        nxt = None if last else (norm_pre[l + 1, 0], mod[l + 1], 0)
        x, h = _rowout(up, ffn_w_out, (l, 1), x, mod[l], 2, 0.5, norm_post[l, 2], dims, nxt, split_out=last)

    head_shape = (n_ctx, depth, ctx_seq, NA_HEADS, NA_HEAD_DIM)
    return (x[0].reshape(n_ctx, ctx_seq, d), x[1].reshape(n_lat, lat_seq, d),
            new_k.reshape(head_shape), new_v.reshape(head_shape), new_s)
```

```python
import functools

import jax
import jax.numpy as jnp
from jax import lax
from jax.experimental import pallas as pl
from jax.experimental.pallas import tpu as pltpu

D_MODEL = 2048
GRID_W = 64
RET_HEADS = 8
RET_DK = 128
RET_DV = 256
CHUNK = 128
GM_GROUPS = 4
GM_WIDTH = 1024
GM_GROUP_DIM = GM_WIDTH // GM_GROUPS
NA_HEADS = 8
NA_HEAD_DIM = 128
NA_WIN_ROWS = 8
NA_WIN_COLS = 16
N_SUB = 3
ROPE_BASE = 10000.0
NORM_EPS = 1e-6
NEG_INF = -1e30
NA_W = NA_HEADS * NA_HEAD_DIM
RPB_ROWS = 2 * NA_WIN_ROWS - 1
RPB_COLS = 2 * NA_WIN_COLS - 1

BF16 = jnp.bfloat16
F32 = jnp.float32

V7X_VMEM_LIMIT_BYTES = 56 * 1024 * 1024
ROW_TILE = 512
WIDE_ROW_TILE = 1024
PROJ_COL_TILE = 1024
MOD_ROWS = 16
T_QA, T_KA, T_VA, T_GA, T_UB, T_VB, T_QC, T_KC, T_VC, T_GATES, T_END = 0, 1, 2, 4, 6, 7, 8, 9, 10, 11, 17


def _cparams(n_axes):
    return pltpu.CompilerParams(dimension_semantics=("arbitrary",) * n_axes,
                                vmem_limit_bytes=V7X_VMEM_LIMIT_BYTES)


def _silu(x):
    return x * jax.nn.sigmoid(x)


def _dot(a, b):
    return jnp.dot(a, b, preferred_element_type=F32)


def _dot_nt(a, b):
    return lax.dot_general(a, b, (((1,), (1,)), ((), ())), preferred_element_type=F32)


def _dot_tn(a, b):
    return lax.dot_general(a, b, (((0,), (0,)), ((), ())), preferred_element_type=F32)


def _rms(x):
    return x * lax.rsqrt(jnp.mean(x * x, axis=-1, keepdims=True) + NORM_EPS)


def _modulated_input(x, g_pre, scale, shift):
    return (_rms(x) * g_pre) * (1.0 + scale) + shift


def _adaln_kernel(c_ref, w_ref, b_ref, o_ref):
    s = _silu(c_ref[...]).astype(BF16)
    o_ref[...] = _dot(s, w_ref[...].astype(BF16)) + b_ref[...]


def _adaln(cond, w_ada, b_ada):
    depth, d, n = w_ada.shape
    tn = 1024
    return pl.pallas_call(
        _adaln_kernel,
        grid=(depth, n // tn),
        in_specs=[pl.BlockSpec((MOD_ROWS, d), lambda l, j: (0, 0)),
                  pl.BlockSpec((None, d, tn), lambda l, j: (l, 0, j)),
                  pl.BlockSpec((None, 1, tn), lambda l, j: (l, 0, j))],
        out_specs=pl.BlockSpec((None, MOD_ROWS, tn), lambda l, j: (l, 0, j)),
        out_shape=jax.ShapeDtypeStruct((depth, MOD_ROWS, n), F32),
        compiler_params=_cparams(2),
    )(cond, w_ada, b_ada.reshape(depth, 1, n))


def _mod_spec(dims, tm, sub, part, tile_of_step=lambda s: s):
    n_ctx_rows, lat_seq = dims
    nc = n_ctx_rows // tm
    per = lat_seq // tm

    def index(s, *_):
        i = tile_of_step(s)
        row = jnp.where(i < nc, 0, 1 + (i - nc) // per)
        return (row * (3 * N_SUB) + 3 * sub + part, 0, 0)

    return pl.BlockSpec((None, 1, D_MODEL), index)


def _vec_spec():
    return pl.BlockSpec((1, D_MODEL), lambda *_: (0, 0))


def _split_row_specs(dims, tm, tile_of_step=lambda s: s):
    nc = dims[0] // tm
    ctx = pl.BlockSpec((tm, D_MODEL), lambda s, *_: (jnp.minimum(tile_of_step(s), nc - 1), 0))
    lat = pl.BlockSpec((tm, D_MODEL), lambda s, *_: (jnp.maximum(tile_of_step(s) - nc, 0), 0))
    return ctx, lat


def _prenorm_kernel(xc_ref, xl_ref, g_ref, sc_ref, sh_ref, h_ref, *, nc):
    x = jnp.where(pl.program_id(0) < nc, xc_ref[...], xl_ref[...])
    h_ref[...] = _modulated_input(x, g_ref[...], sc_ref[...], sh_ref[...]).astype(BF16)


def _prenorm(x_ctx, x_lat, g_pre, mod_tab, dims):
    rows = x_ctx.shape[0] + x_lat.shape[0]
    tm = ROW_TILE
    ctx_spec, lat_spec = _split_row_specs(dims, tm)
    return pl.pallas_call(
        functools.partial(_prenorm_kernel, nc=dims[0] // tm),
        grid=(rows // tm,),
        in_specs=[ctx_spec, lat_spec, _vec_spec(), _mod_spec(dims, tm, 0, 1), _mod_spec(dims, tm, 0, 0)],
        out_specs=pl.BlockSpec((tm, D_MODEL), lambda i: (i, 0)),
        out_shape=jax.ShapeDtypeStruct((rows, D_MODEL), BF16),
        compiler_params=_cparams(1),
    )(x_ctx, x_lat, g_pre.reshape(1, D_MODEL), mod_tab, mod_tab)


def _ffn_up_kernel(h_ref, wa_ref, wb_ref, o_ref, wa_s, wb_s):
    @pl.when(pl.program_id(1) == 0)
    def _():
        wa_s[...] = wa_ref[...].astype(BF16)
        wb_s[...] = wb_ref[...].astype(BF16)

    h = h_ref[...]
    a = _dot(h, wa_s[...])
    b = _dot(h, wb_s[...])
    o_ref[...] = (_silu(a) * b).astype(BF16)


def _ffn_up(h, ffn_w_in, layer, which):
    rows, d = h.shape
    f = ffn_w_in.shape[-1] // 2
    tm, tf = WIDE_ROW_TILE, 512
    nf = f // tf
    return pl.pallas_call(
        _ffn_up_kernel,
        grid=(nf, rows // tm),
        in_specs=[pl.BlockSpec((tm, d), lambda j, i: (i, 0)),
                  pl.BlockSpec((None, None, d, tf), lambda j, i: (layer, which, 0, j)),
                  pl.BlockSpec((None, None, d, tf), lambda j, i: (layer, which, 0, nf + j))],
        out_specs=pl.BlockSpec((tm, tf), lambda j, i: (i, j)),
        out_shape=jax.ShapeDtypeStruct((rows, f), BF16),
        scratch_shapes=[pltpu.VMEM((d, tf), BF16), pltpu.VMEM((d, tf), BF16)],
        compiler_params=_cparams(2),
    )(h, ffn_w_in, ffn_w_in)


def _rowout_kernel(*refs, res_w, n_load, tk, nc, n_sub, split_in, split_out, with_next):
    refs = list(refs)
    a_ref, w_ref = refs[:2]
    del refs[:2]
    x_refs = [refs.pop(0) for _ in range(2 if split_in else 1)]
    gpost_ref, gate_ref = refs[:2]
    del refs[:2]
    if with_next:
        gpre_ref, sc_ref, sh_ref = refs[:3]
        del refs[:3]
    xo_refs = [refs.pop(0) for _ in range(2 if split_out else 1)]
    ho_ref = refs.pop(0) if with_next else None
    (w_s,) = refs
    s = pl.program_id(0)

    @pl.when(s < n_load)
    def _():
        w_s[pl.ds(pl.multiple_of(s * tk, tk), tk), :] = w_ref[...].astype(BF16)

    @pl.when(s >= n_load)
    def _():
        i = s - n_load
        sub = a_ref.shape[0] // n_sub
        for r in range(n_sub):
            rows = slice(r * sub, (r + 1) * sub)
            y = _dot(a_ref[rows, :], w_s[...])
            x = jnp.where(i < nc, x_refs[0][rows, :], x_refs[1][rows, :]) if split_in else x_refs[0][rows, :]
            xn = x + (res_w * gate_ref[...]) * (_rms(y) * gpost_ref[...])
            if split_out:
                @pl.when(i < nc)
                def _():
                    xo_refs[0][rows, :] = xn

                @pl.when(i >= nc)
                def _():
                    xo_refs[1][rows, :] = xn
            else:
                xo_refs[0][rows, :] = xn
            if with_next:
                ho_ref[rows, :] = _modulated_input(xn, gpre_ref[...], sc_ref[...], sh_ref[...]).astype(BF16)


def _rowout(a, w_all, w_index, x, mod_tab, sub, res_w, g_post, dims, nxt=None, split_out=False):
    rows, kdim = a.shape
    tm = ROW_TILE if kdim <= D_MODEL else ROW_TILE // 2
    tk = 256
    n_load = kdim // tk
    nc = dims[0] // tm
    split_in = isinstance(x, tuple)

    def tile(s):
        return jnp.maximum(s - n_load, 0)

    row_spec = pl.BlockSpec((tm, D_MODEL), lambda s: (tile(s), 0))
    lead = (None,) * len(w_index)
    in_specs = [pl.BlockSpec((tm, kdim), lambda s: (tile(s), 0)),
                pl.BlockSpec(lead + (tk, D_MODEL), lambda s: tuple(w_index) + (jnp.minimum(s, n_load - 1), 0))]
    args = [a, w_all]
    if split_in:
        in_specs += list(_split_row_specs(dims, tm, tile))
        args += list(x)
    else:
        in_specs.append(row_spec)
        args.append(x)
    in_specs += [_vec_spec(), _mod_spec(dims, tm, sub, 2, tile)]
    args += [g_post.reshape(1, D_MODEL), mod_tab]
    if nxt is not None:
        g_pre, mod_next, sub_next = nxt
        in_specs += [_vec_spec(), _mod_spec(dims, tm, sub_next, 1, tile), _mod_spec(dims, tm, sub_next, 0, tile)]
        args += [g_pre.reshape(1, D_MODEL), mod_next, mod_next]
    if split_out:
        out_specs = list(_split_row_specs(dims, tm, tile))
        out_shape = [jax.ShapeDtypeStruct((dims[0], D_MODEL), F32),
                     jax.ShapeDtypeStruct((rows - dims[0], D_MODEL), F32)]
    else:
        out_specs = [row_spec]
        out_shape = [jax.ShapeDtypeStruct((rows, D_MODEL), F32)]
    if nxt is not None:
        out_specs.append(row_spec)
        out_shape.append(jax.ShapeDtypeStruct((rows, D_MODEL), BF16))
    out = pl.pallas_call(
        functools.partial(_rowout_kernel, res_w=res_w, n_load=n_load, tk=tk, nc=nc, split_in=split_in,
                          n_sub=2, split_out=split_out, with_next=nxt is not None),
        grid=(n_load + rows // tm,),
        in_specs=in_specs,
        out_specs=out_specs,
        out_shape=out_shape,
        scratch_shapes=[pltpu.VMEM((kdim, D_MODEL), BF16)],
        compiler_params=_cparams(1),
    )(*args)
    n_x = 2 if split_out else 1
    x_new = tuple(out[:n_x]) if split_out else out[0]
    return x_new, (out[n_x] if nxt is not None else None)


def _proj_kernel(*refs):
    h_ref, w_ref = refs[:2]
    o_ref, w_s = refs[-2:]

    @pl.when(pl.program_id(1) == 0)
    def _():
        w_s[...] = w_ref[...].astype(BF16)

    o_ref[...] = _dot(h_ref[...], w_s[...]).astype(o_ref.dtype).reshape(o_ref.shape)


def _proj(h, w_in, layer, tile0, n_tiles, out_dtype, row0=0, n_rows=None):
    d = h.shape[1]
    tm, tn = WIDE_ROW_TILE, PROJ_COL_TILE
    n_row_tiles = (h.shape[0] if n_rows is None else n_rows) // tm
    row_tile0 = row0 // tm
    return pl.pallas_call(
        _proj_kernel,
        grid=(n_tiles, n_row_tiles),
        in_specs=[pl.BlockSpec((tm, d), lambda j, i: (row_tile0 + i, 0)),
                  pl.BlockSpec((None, d, tn), lambda j, i: (layer, 0, tile0 + j))],
        out_specs=pl.BlockSpec((tm, tn), lambda j, i: (i, j)),
        out_shape=jax.ShapeDtypeStruct((n_row_tiles * tm, n_tiles * tn), out_dtype),
        scratch_shapes=[pltpu.VMEM((d, tn), BF16)],
        compiler_params=_cparams(2),
    )(h, w_in)


def _proj_cache(h, w_in, layer, tile, n_seq, seq, depth, prev):
    d = h.shape[1]
    tm, tn = ROW_TILE, PROJ_COL_TILE
    per = tm // seq
    in_specs = [pl.BlockSpec((tm, d), lambda j, i: (i, 0)),
                pl.BlockSpec((None, d, tn), lambda j, i: (layer, 0, tile))]
    args = [h, w_in]
    aliases = {}
    if prev is not None:
        in_specs.append(pl.BlockSpec(memory_space=pl.ANY))
        args.append(prev)
        aliases = {2: 0}
    return pl.pallas_call(
        _proj_kernel,
        grid=(1, n_seq // per),
        in_specs=in_specs,
        out_specs=pl.BlockSpec((per, None, seq, tn), lambda j, i: (i, layer, 0, 0)),
        out_shape=jax.ShapeDtypeStruct((n_seq, depth, seq, tn), F32),
        scratch_shapes=[pltpu.VMEM((d, tn), BF16)],
        input_output_aliases=aliases,
        compiler_params=_cparams(2),
    )(*args)


def _log_sigmoid(x):
    return jnp.minimum(x, 0.0) - jnp.log1p(jnp.exp(-jnp.abs(x)))


def _retention_kernel(*refs, n_chunks, latent, hp):
    if latent:
        (logit_ref, q_ref, k_ref, v_ref, g_ref, cos_ref, sa_ref, sb_ref, s0_ref, _,
         o_ref, q_s, k_s, kf_s, kb_s, qx_s, sf_s, sb_s) = refs
    else:
        (logit_ref, q_ref, k_ref, v_ref, g_ref) = refs[:5]
        (o_ref, so_ref, q_s, k_s, kf_s, kb_s, qx_s, sf_s, sb_s) = refs[-9:]
    c = CHUNK
    ii = lax.broadcasted_iota(jnp.int32, (c, c), 0)
    jj = lax.broadcasted_iota(jnp.int32, (c, c), 1)
    dist = (ii - jj).astype(F32)
    pos = ii.astype(F32)

    for j in range(hp):
        qcols = slice(j * RET_DK, (j + 1) * RET_DK)
        vcols = slice(j * RET_DV, (j + 1) * RET_DV)
        lg = _log_sigmoid(logit_ref[j])
        lgf, lgb = lg[0], lg[1]
        lgf_k, lgb_k = lgf[:, :RET_DK], lgb[:, :RET_DK]
        dmask = (jnp.where(dist >= 0, jnp.exp(lgf_k * jnp.maximum(dist, 0.0)), 0.0)
                 + jnp.where(dist <= 0, jnp.exp(lgb_k * jnp.maximum(-dist, 0.0)), 0.0))
        zeta_f = jnp.exp(lgf_k * (c - 1.0 - pos))
        zeta_b = jnp.exp(lgb_k * pos)
        xi_f = jnp.exp(lgf_k * (pos + 1.0))
        xi_b = jnp.exp(lgb_k * (c - pos))
        decay_f = jnp.exp(lgf * c)
        decay_b = jnp.exp(lgb * c)

        for n in range(n_chunks):
            sl = slice(n * c, (n + 1) * c)
            q = q_ref[sl, qcols].astype(F32)
            k = k_ref[sl, qcols].astype(F32)
            if latent:
                cos, sa, sb = cos_ref[sl, :], sa_ref[sl, :], sb_ref[sl, :]
                q = q * cos + pltpu.roll(q, 96, 1) * sa + pltpu.roll(q, 32, 1) * sb
                k = k * cos + pltpu.roll(k, 96, 1) * sa + pltpu.roll(k, 32, 1) * sb
            k = k * (RET_DK ** -0.5)
            q_s[j, sl, :] = q.astype(BF16)
            k_s[j, sl, :] = k.astype(BF16)
            kf_s[j, sl, :] = (k * zeta_f).astype(BF16)
            kb_s[j, sl, :] = (k * zeta_b).astype(BF16)
            qx_s[j, sl, :RET_DK] = (q * xi_f).astype(BF16)
            qx_s[j, sl, RET_DK:] = (q * xi_b).astype(BF16)

        s = s0_ref[0, j] if latent else jnp.zeros((RET_DK, RET_DV), F32)
        for n in range(n_chunks):
            sl = slice(n * c, (n + 1) * c)
            sf_s[j, n] = s.astype(BF16)
            s = decay_f * s + _dot_tn(kf_s[j, sl, :], v_ref[sl, vcols])
        if not latent:
            so_ref[0, j] = s
        s = s0_ref[1, j] if latent else jnp.zeros((RET_DK, RET_DV), F32)
        for n in reversed(range(n_chunks)):
            sl = slice(n * c, (n + 1) * c)
            sb_s[j, n] = s.astype(BF16)
            s = decay_b * s + _dot_tn(kb_s[j, sl, :], v_ref[sl, vcols])
        if not latent:
            so_ref[1, j] = s

        for n in range(n_chunks):
            sl = slice(n * c, (n + 1) * c)
            scores = (_dot_nt(q_s[j, sl, :], k_s[j, sl, :]) * dmask).astype(BF16)
            lhs = jnp.concatenate([scores, qx_s[j, sl, :]], axis=1)
            rhs = jnp.concatenate([v_ref[sl, vcols], sf_s[j, n], sb_s[j, n]], axis=0)
            o = _rms(_dot(lhs, rhs)) * _silu(g_ref[sl, vcols].astype(F32))
            o_ref[sl, vcols] = o.astype(BF16)


def _retention_scratch(seq, hp):
    n_chunks = seq // CHUNK
    return ([pltpu.VMEM((hp, seq, RET_DK), BF16)] * 4 + [pltpu.VMEM((hp, seq, 2 * RET_DK), BF16)]
            + [pltpu.VMEM((hp, n_chunks, RET_DK, RET_DV), BF16)] * 2)


def _retention_in_specs(seq, r0, hp):
    kq, kk = T_QA * PROJ_COL_TILE // (hp * RET_DK), T_KA * PROJ_COL_TILE // (hp * RET_DK)
    kv, kg = T_VA * PROJ_COL_TILE // (hp * RET_DV), T_GA * PROJ_COL_TILE // (hp * RET_DV)
    return [pl.BlockSpec((hp, 2, 1, RET_DV), lambda b, h: (h, 0, 0, 0)),
            pl.BlockSpec((seq, hp * RET_DK), lambda b, h: (r0 + b, kq + h)),
            pl.BlockSpec((seq, hp * RET_DK), lambda b, h: (r0 + b, kk + h)),
            pl.BlockSpec((seq, hp * RET_DV), lambda b, h: (r0 + b, kv + h)),
            pl.BlockSpec((seq, hp * RET_DV), lambda b, h: (r0 + b, kg + h))]


RET_CTX_HEADS_PER_STEP = 8
RET_LAT_HEADS_PER_STEP = 2


def _retention_ctx(z, logits, n_seq, seq, layer, depth, prev_states):
    rows = z.shape[0]
    hp = RET_CTX_HEADS_PER_STEP
    in_specs = _retention_in_specs(seq, 0, hp)
    args = [logits, z, z, z, z]
    aliases = {}
    if prev_states is not None:
        in_specs.append(pl.BlockSpec(memory_space=pl.ANY))
        args.append(prev_states)
        aliases = {5: 1}
    return pl.pallas_call(
        functools.partial(_retention_kernel, n_chunks=seq // CHUNK, latent=False, hp=hp),
        grid=(n_seq, RET_HEADS // hp),
        in_specs=in_specs,
        out_specs=[pl.BlockSpec((seq, hp * RET_DV), lambda b, h: (b, h)),
                   pl.BlockSpec((None, None, 2, hp, RET_DK, RET_DV), lambda b, h: (b, layer, 0, h, 0, 0))],
        out_shape=[jax.ShapeDtypeStruct((rows, RET_HEADS * RET_DV), BF16),
                   jax.ShapeDtypeStruct((n_seq, depth, 2, RET_HEADS, RET_DK, RET_DV), F32)],
        scratch_shapes=_retention_scratch(seq, hp),
        input_output_aliases=aliases,
        compiler_params=_cparams(2),
    )(*args)


def _retention_latent(z, logits, rot, state, layer, ret_all, n_seq, seq, row0):
    r0 = row0 // seq
    hp = RET_LAT_HEADS_PER_STEP
    tab_spec = pl.BlockSpec((seq, RET_DK), lambda b, h: (0, 0))
    return pl.pallas_call(
        functools.partial(_retention_kernel, n_chunks=seq // CHUNK, latent=True, hp=hp),
        grid=(n_seq, RET_HEADS // hp),
        in_specs=_retention_in_specs(seq, r0, hp) + [
            tab_spec, tab_spec, tab_spec,
            pl.BlockSpec((None, None, 2, hp, RET_DK, RET_DV), lambda b, h: (b, layer, 0, h, 0, 0)),
            pl.BlockSpec(memory_space=pl.ANY)],
        out_specs=pl.BlockSpec((seq, hp * RET_DV), lambda b, h: (r0 + b, h)),
        out_shape=jax.ShapeDtypeStruct(ret_all.shape, BF16),
        scratch_shapes=_retention_scratch(seq, hp),
        input_output_aliases={9: 0},
        compiler_params=_cparams(2),
    )(logits, z, z, z, z, rot[0], rot[1], rot[2], state, ret_all)


def _gmlp_kernel(u_ref, v_ref, gn_ref, ws_ref, bs_ref, o_ref, *, n_chunks):
    v = v_ref[...].astype(F32)
    mu = jnp.mean(v, axis=-1, keepdims=True)
    vc = v - mu
    var = jnp.mean(vc * vc, axis=-1, keepdims=True)
    vn = (vc * lax.rsqrt(var + NORM_EPS) * gn_ref[...]).astype(BF16)
    for g in range(GM_GROUPS):
        w = ws_ref[g].astype(BF16)
        cols = slice(g * GM_GROUP_DIM, (g + 1) * GM_GROUP_DIM)
        for n in range(n_chunks):
            rows = slice(n * CHUNK, (n + 1) * CHUNK)
            mixed = _dot(w, vn[rows, cols]) + bs_ref[g]
            o_ref[rows, cols] = (u_ref[rows, cols].astype(F32) * mixed).astype(BF16)


def _gmlp(z, gm_norm, gm_ws, gm_bs):
    rows = z.shape[0]
    tm = ROW_TILE
    return pl.pallas_call(
        functools.partial(_gmlp_kernel, n_chunks=tm // CHUNK),
        grid=(rows // tm,),
        in_specs=[pl.BlockSpec((tm, GM_WIDTH), lambda i: (i, T_UB * PROJ_COL_TILE // GM_WIDTH)),
                  pl.BlockSpec((tm, GM_WIDTH), lambda i: (i, T_VB * PROJ_COL_TILE // GM_WIDTH)),
                  pl.BlockSpec((1, GM_WIDTH), lambda i: (0, 0)),
                  pl.BlockSpec((GM_GROUPS, CHUNK, CHUNK), lambda i: (0, 0, 0)),
                  pl.BlockSpec((GM_GROUPS, CHUNK, 1), lambda i: (0, 0, 0))],
        out_specs=pl.BlockSpec((tm, GM_WIDTH), lambda i: (i, 0)),
        out_shape=jax.ShapeDtypeStruct((rows, GM_WIDTH), BF16),
        compiler_params=_cparams(1),
    )(z, z, gm_norm.reshape(1, GM_WIDTH), gm_ws, gm_bs.reshape(GM_GROUPS, CHUNK, 1))


def _softmax_rows(parts):
    m = functools.reduce(jnp.maximum, [jnp.max(s, axis=-1, keepdims=True) for s in parts])
    e = [jnp.exp(s - m) for s in parts]
    denom = functools.reduce(lambda a, b: a + b, [jnp.sum(x, axis=-1, keepdims=True) for x in e])
    return [x.astype(BF16) for x in e], denom


def _attn_ctx_kernel(q_ref, k_ref, v_ref, o_ref):
    scale = NA_HEAD_DIM ** -0.5
    for h in range(NA_HEADS):
        cols = slice(h * NA_HEAD_DIM, (h + 1) * NA_HEAD_DIM)
        (p,), denom = _softmax_rows([_dot_nt(q_ref[:, cols], k_ref[:, cols].astype(BF16)) * scale])
        o_ref[:, cols] = (_dot(p, v_ref[:, cols].astype(BF16)) / denom).astype(BF16)


def _attn_ctx(z, new_k, new_v, layer, n_seq, seq):
    rows = z.shape[0]
    cache_spec = pl.BlockSpec((None, None, seq, NA_W), lambda b: (b, layer, 0, 0))
    return pl.pallas_call(
        _attn_ctx_kernel,
        grid=(n_seq,),
        in_specs=[pl.BlockSpec((seq, NA_W), lambda b: (b, T_QC * PROJ_COL_TILE // NA_W)), cache_spec, cache_spec],
        out_specs=pl.BlockSpec((seq, NA_W), lambda b: (b, 0)),
        out_shape=jax.ShapeDtypeStruct((rows, NA_W), BF16),
        compiler_params=_cparams(1),
    )(z, new_k, new_v)


def _window_rows(seq):
    rows = seq // GRID_W
    wr = min(NA_WIN_ROWS, rows)
    return rows, wr, [min(max(r - wr // 2, 0), rows - wr) for r in range(rows)]


def _build_window_bias(rpb_ref, base, bias_s, seq):
    rows, wr, row_start = _window_rows(seq)
    shape = (GRID_W, 2 * GRID_W)
    qc = lax.broadcasted_iota(jnp.int32, shape, 0)
    lane = lax.broadcasted_iota(jnp.int32, shape, 1)
    upper = lane >= GRID_W
    kc = jnp.where(upper, lane - GRID_W, lane)
    col_start = jnp.clip(qc - NA_WIN_COLS // 2, 0, GRID_W - NA_WIN_COLS)
    col_ok = (kc >= col_start) & (kc < col_start + NA_WIN_COLS)
    col_off = jnp.clip(kc - qc + NA_WIN_COLS - 1, 0, RPB_COLS - 1)
    tabs = [jnp.zeros(shape, F32)] * RPB_ROWS
    for c in range(RPB_COLS):
        hit = col_off == c
        tabs = [jnp.where(hit, rpb_ref[base + dr * RPB_COLS + c], t) for dr, t in enumerate(tabs)]
    neg = jnp.full(shape, NEG_INF, F32)
    tabs = [jnp.where(col_ok, t, neg) for t in tabs]
    for qr in range(rows):
        for m in range(rows // 2):
            halves = []
            for kr in (2 * m, 2 * m + 1):
                inside = row_start[qr] <= kr < row_start[qr] + wr
                halves.append(tabs[kr - qr + NA_WIN_ROWS - 1] if inside else neg)
            blk = halves[0] if halves[0] is halves[1] else jnp.where(upper, halves[1], halves[0])
            bias_s[qr * GRID_W:(qr + 1) * GRID_W, m * 2 * GRID_W:(m + 1) * 2 * GRID_W] = blk


def _attn_lat_kernel(rpb_ref, q_ref, k_ref, v_ref, kc_ref, vc_ref, _, o_ref, bias_s, *, q_block):
    seq = q_ref.shape[0]

    @pl.when(pl.program_id(1) == 0)
    def _():
        _build_window_bias(rpb_ref, pl.program_id(0) * (RPB_ROWS * RPB_COLS), bias_s, seq)

    scale = NA_HEAD_DIM ** -0.5
    kc = kc_ref[...].astype(BF16)
    vc = vc_ref[...].astype(BF16)
    _, wr, row_start = _window_rows(seq)
    rows_per_block = q_block // GRID_W
    for n in range(seq // q_block):
        lo = row_start[n * rows_per_block] * GRID_W
        hi = (row_start[(n + 1) * rows_per_block - 1] + wr) * GRID_W
        rows = slice(n * q_block, (n + 1) * q_block)
        q = q_ref[rows, :]
        s_loc = _dot_nt(q, k_ref[lo:hi, :]) * scale + bias_s[rows, lo:hi]
        s_ctx = _dot_nt(q, kc) * scale
        (p_loc, p_ctx), denom = _softmax_rows([s_loc, s_ctx])
        o_ref[rows, :] = ((_dot(p_loc, v_ref[lo:hi, :]) + _dot(p_ctx, vc)) / denom).astype(BF16)


def _attn_latent(z, kv_lat, cache_k, cache_v, rpb, layer, na_all, n_seq, seq, row0):
    r0 = row0 // seq
    past = cache_k.shape[2]
    kq = T_QC * PROJ_COL_TILE // NA_HEAD_DIM
    cache_spec = pl.BlockSpec((None, None, past, NA_HEAD_DIM), lambda h, b: (b, layer, 0, h))
    return pl.pallas_call(
        functools.partial(_attn_lat_kernel, q_block=256),
        grid=(NA_HEADS, n_seq),
        in_specs=[pl.BlockSpec(memory_space=pltpu.SMEM),
                  pl.BlockSpec((seq, NA_HEAD_DIM), lambda h, b: (r0 + b, kq + h)),
                  pl.BlockSpec((seq, NA_HEAD_DIM), lambda h, b: (b, h)),
                  pl.BlockSpec((seq, NA_HEAD_DIM), lambda h, b: (b, NA_HEADS + h)),
                  cache_spec, cache_spec,
                  pl.BlockSpec(memory_space=pl.ANY)],
        out_specs=pl.BlockSpec((seq, NA_HEAD_DIM), lambda h, b: (r0 + b, h)),
        out_shape=jax.ShapeDtypeStruct(na_all.shape, BF16),
        scratch_shapes=[pltpu.VMEM((seq, seq), F32)],
        input_output_aliases={6: 0},
        compiler_params=_cparams(2),
    )(rpb.reshape(-1), z, kv_lat, kv_lat,
      cache_k.reshape(cache_k.shape[:3] + (NA_W,)), cache_v.reshape(cache_v.shape[:3] + (NA_W,)), na_all)


def _rotary_tables(seq):
    t = jnp.arange(seq)
    half = RET_DK // 2
    inv = ROPE_BASE ** (-jnp.arange(0, half, 2, dtype=F32) / half)
    ang_r = (t // GRID_W).astype(F32)[:, None] * inv[None, :]
    ang_c = (t % GRID_W).astype(F32)[:, None] * inv[None, :]
    zero = jnp.zeros_like(ang_r)
    cos = jnp.concatenate([jnp.cos(ang_r), jnp.cos(ang_r), jnp.cos(ang_c), jnp.cos(ang_c)], axis=-1)
    sa = jnp.concatenate([-jnp.sin(ang_r), zero, -jnp.sin(ang_c), zero], axis=-1)
    sb = jnp.concatenate([zero, jnp.sin(ang_r), zero, jnp.sin(ang_c)], axis=-1)
    return cos, sa, sb


def _merge_kernel(ret_ref, gm_ref, na_ref, ga_ref, gb_ref, gc_ref, wr_ref, wg_ref, wn_ref, o_ref, wr_s, wg_s, wn_s):
    @pl.when(pl.program_id(1) == 0)
    def _():
        wr_s[...] = wr_ref[...].astype(BF16)
        wg_s[...] = wg_ref[...].astype(BF16)
        wn_s[...] = wn_ref[...].astype(BF16)

    m = (jax.nn.sigmoid(ga_ref[...].astype(F32)) * _dot(ret_ref[...], wr_s[...])
         + jax.nn.sigmoid(gb_ref[...].astype(F32)) * _dot(gm_ref[...], wg_s[...])
         + jax.nn.sigmoid(gc_ref[...].astype(F32)) * _dot(na_ref[...], wn_s[...]))
    o_ref[...] = m.astype(BF16)


def _merge(ret, gm, na, gates, w_br, w_bg, w_bn, layer):
    rows = ret.shape[0]
    tm, tn = ROW_TILE, 512
    nj = D_MODEL // tn

    def a_spec(width):
        return pl.BlockSpec((tm, width), lambda j, i: (i, 0))

    def gate_spec(which):
        return pl.BlockSpec((tm, tn), lambda j, i: (i, which * nj + j))

    def w_spec(width):
        return pl.BlockSpec((None, width, tn), lambda j, i: (layer, 0, j))

    return pl.pallas_call(
        _merge_kernel,
        grid=(nj, rows // tm),
        in_specs=[a_spec(ret.shape[1]), a_spec(gm.shape[1]), a_spec(na.shape[1]),
                  gate_spec(0), gate_spec(1), gate_spec(2),
                  w_spec(ret.shape[1]), w_spec(gm.shape[1]), w_spec(na.shape[1])],
        out_specs=pl.BlockSpec((tm, tn), lambda j, i: (i, j)),
        out_shape=jax.ShapeDtypeStruct((rows, D_MODEL), BF16),
        scratch_shapes=[pltpu.VMEM((ret.shape[1], tn), BF16), pltpu.VMEM((gm.shape[1], tn), BF16),
                        pltpu.VMEM((na.shape[1], tn), BF16)],
        compiler_params=_cparams(2),
    )(ret, gm, na, gates, gates, gates, w_br, w_bg, w_bn)


def kernel(x_prompt, x_sample, cache_k, cache_v, state_ret, c, c_ctx, w_ada, b_ada, norm_pre, norm_post, ffn_w_in, ffn_w_out, w_in, ret_decay_logit, gm_norm, gm_ws, gm_bs, na_rpb, w_branch_ret, w_branch_gm, w_branch_na, w_out):
    n_ctx, ctx_seq, d = x_prompt.shape
    n_lat, lat_seq, _ = x_sample.shape
    depth = w_ada.shape[0]
    n_ctx_rows = n_ctx * ctx_seq
    dims = (n_ctx_rows, lat_seq)

    x = (x_prompt.reshape(n_ctx_rows, d), x_sample.reshape(n_lat * lat_seq, d))
    cond = jnp.concatenate([c_ctx[None, :], c, jnp.zeros((MOD_ROWS - 1 - n_lat, d), F32)], axis=0)
    mod = _adaln(cond, w_ada, b_ada).reshape(depth, MOD_ROWS * 3 * N_SUB, 1, d)
    rot = _rotary_tables(lat_seq)

    h = _prenorm(x[0], x[1], norm_pre[0, 0], mod[0], dims)
    new_k = new_v = new_s = None
    for l in range(depth):
        up = _ffn_up(h, ffn_w_in, l, 0)
        x, h = _rowout(up, ffn_w_out, (l, 0), x, mod[l], 0, 0.5, norm_post[l, 0], dims, (norm_pre[l, 1], mod[l], 1))

        z = _proj(h, w_in, l, T_QA, T_KC - T_QA, BF16)
        new_k = _proj_cache(h, w_in, l, T_KC, n_ctx, ctx_seq, depth, new_k)
        new_v = _proj_cache(h, w_in, l, T_VC, n_ctx, ctx_seq, depth, new_v)
        kv_lat = _proj(h, w_in, l, T_KC, T_GATES - T_KC, BF16, n_ctx_rows, n_lat * lat_seq)
        gates = _proj(h, w_in, l, T_GATES, T_END - T_GATES, BF16)

        logits = jnp.broadcast_to(ret_decay_logit[l].T[:, :, None, None], (RET_HEADS, 2, 1, RET_DV))
        ret, new_s = _retention_ctx(z, logits, n_ctx, ctx_seq, l, depth, new_s)
        ret = _retention_latent(z, logits, rot, state_ret, l, ret, n_lat, lat_seq, n_ctx_rows)
        gm = _gmlp(z, gm_norm[l], gm_ws[l], gm_bs[l])
        na = _attn_ctx(z, new_k, new_v, l, n_ctx, ctx_seq)
        na = _attn_latent(z, kv_lat, cache_k, cache_v, na_rpb[l], l, na, n_lat, lat_seq, n_ctx_rows)
        m = _merge(ret, gm, na, gates, w_branch_ret, w_branch_gm, w_branch_na, l)
        x, h = _rowout(m, w_out, (l,), x, mod[l], 1, 1.0, norm_post[l, 1], dims, (norm_pre[l, 2], mod[l], 2))

        up = _ffn_up(h, ffn_w_in, l, 1)
        last = l + 1 == depth
        nxt = None if last else (norm_pre[l + 1, 0], mod[l + 1], 0)
        x, h = _rowout(up, ffn_w_out, (l, 1), x, mod[l], 2, 0.5, norm_post[l, 2], dims, nxt, split_out=last)

    head_shape = (n_ctx, depth, ctx_seq, NA_HEADS, NA_HEAD_DIM)
    return (x[0].reshape(n_ctx, ctx_seq, d), x[1].reshape(n_lat, lat_seq, d),
            new_k.reshape(head_shape), new_v.reshape(head_shape), new_s)
```
